```python
import math
import jax, jax.numpy as jnp
from jax import lax
import numpy as np

D_MODEL = 1024
BATCH = 8
SEQ = 8192
DEPTH = 2

CTX_LEN = 256
GRID_W = 64
EPS = 1e-6
ROPE_BASE = 10000.0
Q_BLOCK = 128
f32 = jnp.float32

H_A = 4
DH_A = 64
DV_A = 2 * DH_A
A_W = H_A * DV_A
H_B = 4
DH_B = 128
B_W = H_B * DH_B
GDN_CHUNK = 64
CONV_K = 5
H_C = 8
DH_C = 128
WIN_R = 8
WIN_C = 16
MIX_ODD = H_C * DH_C

MIX_EVEN = A_W + B_W
EVEN_IN = 3 * A_W + 4 * B_W + 4 * H_B
EVEN_CUTS = [A_W, 2 * A_W, 3 * A_W, 3 * A_W + 3 * B_W, 3 * A_W + 4 * B_W]
FF_HIDDEN = ((8 * D_MODEL // 3 + 255) // 256) * 256
N_EVEN = (DEPTH + 1) // 2
N_ODD = DEPTH // 2

kernel_name = "hybrid_diffattn_gdn_natten_trunk"


def rms_norm(x, gain):
    xf = x.astype(f32)
    y = xf * lax.rsqrt(jnp.mean(xf * xf, axis=-1, keepdims=True) + EPS)
    return (y * gain.astype(f32)).astype(x.dtype)


def l2_norm(x):
    return x * lax.rsqrt(jnp.sum(x * x, axis=-1, keepdims=True) + EPS)


def heads(t, n):
    b, l, _ = t.shape
    return t.reshape(b, l, n, -1).transpose(0, 2, 1, 3)


def merge_heads(t):
    b, h, l, d = t.shape
    return t.transpose(0, 2, 1, 3).reshape(b, l, h * d)


def axial_rope(l, dim):
    n_freq = dim // 4
    inv = ROPE_BASE ** (-jnp.arange(n_freq, dtype=f32) / n_freq)
    t = jnp.arange(l)
    row = (t // GRID_W).astype(f32)
    col = (t % GRID_W).astype(f32)
    ang = jnp.concatenate([row[:, None] * inv, col[:, None] * inv], axis=-1)
    return jnp.cos(ang), jnp.sin(ang)


def apply_rope(x, cos, sin):
    xf = x.astype(f32).reshape(*x.shape[:-1], x.shape[-1] // 2, 2)
    x0, x1 = xf[..., 0], xf[..., 1]
    out = jnp.stack([x0 * cos - x1 * sin, x0 * sin + x1 * cos], axis=-1)
    return out.reshape(x.shape).astype(x.dtype)


def softmax_attention(q, k, v):
    s = jnp.einsum('bhqd,bhkd->bhqk', q, k).astype(f32) * (q.shape[-1] ** -0.5)
    return jnp.einsum('bhqk,bhkd->bhqd', jax.nn.softmax(s, axis=-1).astype(v.dtype), v)


def over_query_blocks(fn, qs):
    b, h, l, _ = qs[0].shape
    nb = l // Q_BLOCK
    blocked = tuple(jnp.moveaxis(t.reshape(b, h, nb, Q_BLOCK, t.shape[-1]), 2, 0) for t in qs)
    out = lax.map(lambda qq: fn(*qq), blocked)
    return jnp.moveaxis(out, 0, 2).reshape(b, h, l, out.shape[-1])


def diff_core(q1, q2, k1, k2, v, lam):
    scale = DH_A ** -0.5
    p1 = jax.nn.softmax(jnp.einsum('bhqd,bhkd->bhqk', q1, k1).astype(f32) * scale, axis=-1)
    p2 = jax.nn.softmax(jnp.einsum('bhqd,bhkd->bhqk', q2, k2).astype(f32) * scale, axis=-1)
    return jnp.einsum('bhqk,bhkd->bhqd', (p1 - lam * p2).astype(v.dtype), v)


def diff_attention(q, k, v, qc, kc, vc, qk_gain, lam_vec, subln, lam_init, cos, sin, ctx_out):
    lv = lam_vec.astype(f32)
    lam = jnp.exp(jnp.sum(lv[0] * lv[1])) - jnp.exp(jnp.sum(lv[2] * lv[3])) + lam_init

    def split_maps(t, gain, rope):
        t = heads(t, H_A)
        b, h, l, _ = t.shape
        t = rms_norm(t.reshape(b, h, l, 2, DH_A), gain)
        t1, t2 = t[..., 0, :], t[..., 1, :]
        if rope:
            t1, t2 = apply_rope(t1, cos, sin), apply_rope(t2, cos, sin)
        return t1, t2

    q1, q2 = split_maps(q, qk_gain[0], True)
    k1, k2 = split_maps(k, qk_gain[1], True)
    k1c, k2c = split_maps(kc, qk_gain[1], False)
    vh, vch = heads(v, H_A), heads(vc, H_A)
    k1a = jnp.concatenate([k1c, k1], axis=2)
    k2a = jnp.concatenate([k2c, k2], axis=2)
    va = jnp.concatenate([vch, vh], axis=2)
    o = over_query_blocks(lambda a, b_: diff_core(a, b_, k1a, k2a, va, lam), (q1, q2))

    def post(t):
        return merge_heads(rms_norm(t, subln) * (1.0 - lam_init))

    oc = None
    if ctx_out:
        q1c, q2c = split_maps(qc, qk_gain[0], False)
        oc = post(diff_core(q1c, q2c, k1c, k2c, vch, lam))
    return post(o), oc


def short_conv(t, w):
    c = t.shape[-1]
    y = lax.conv_general_dilated(t, w[:, None, :].astype(t.dtype), window_strides=(1,),
                                 padding=[(CONV_K // 2, CONV_K // 2)],
                                 dimension_numbers=('NWC', 'WIO', 'NWC'), feature_group_count=c)
    return jax.nn.silu(y)


def gdn_inputs(qkv, gates, conv_w, a_log, dt_bias):
    y = short_conv(qkv, conv_w).astype(f32)
    q, k, v = jnp.split(y, 3, axis=-1)
    q = l2_norm(heads(q, H_B)) * (DH_B ** -0.5)
    k = l2_norm(heads(k, H_B))
    v = heads(v, H_B)
    b, l, _ = gates.shape
    g = gates.astype(f32).transpose(0, 2, 1)
    beta = jax.nn.sigmoid(g[:, :2 * H_B]).reshape(b, 2, H_B, l)
    a = g[:, 2 * H_B:].reshape(b, 2, H_B, l)
    log_alpha = -jnp.exp(a_log.astype(f32))[None, :, :, None] * jax.nn.softplus(a + dt_bias.astype(f32)[None, :, :, None])
    return q, k, v, log_alpha, beta


def gated_delta_chunked(q, k, v, log_a, beta, s0):
    b, h, l, dk = q.shape
    dv = v.shape[-1]
    n = l // GDN_CHUNK
    rs = lambda t: t.reshape(b, h, n, GDN_CHUNK, *t.shape[3:])
    q, k, v, log_a, beta = rs(q), rs(k), rs(v), rs(log_a), rs(beta)
    g = jnp.cumsum(log_a, axis=-1)
    idx = jnp.arange(GDN_CHUNK)
    causal = idx[:, None] >= idx[None, :]
    strict = idx[:, None] > idx[None, :]
    decay = jnp.exp(jnp.where(causal, g[..., :, None] - g[..., None, :], -jnp.inf))
    kk = jnp.einsum('bhncd,bhnsd->bhncs', k, k)
    lmat = jnp.where(strict, beta[..., :, None] * decay * kk, 0.0)
    rhs = jnp.concatenate([beta[..., None] * v, (beta * jnp.exp(g))[..., None] * k], axis=-1)
    sol = lax.linalg.triangular_solve(jnp.eye(GDN_CHUNK, dtype=f32) + lmat, rhs,
                                      left_side=True, lower=True, unit_diagonal=True)
    u0, w = sol[..., :dv], sol[..., dv:]
    aqk = jnp.einsum('bhncd,bhnsd->bhncs', q, k) * decay
    qg = q * jnp.exp(g)[..., None]
    kg = k * jnp.exp(g[..., -1:] - g)[..., None]
    gl = jnp.exp(g[..., -1])

    def step(s, inp):
        u0_c, w_c, aqk_c, qg_c, kg_c, gl_c = inp
        u = u0_c - w_c @ s
        o = qg_c @ s + aqk_c @ u
        s = gl_c[..., None, None] * s + jnp.swapaxes(kg_c, -1, -2) @ u
        return s, o

    xs = tuple(jnp.moveaxis(t, 2, 0) for t in (u0, w, aqk, qg, kg, gl))
    s_fin, o = lax.scan(step, s0, xs)
    return jnp.moveaxis(o, 0, 2).reshape(b, h, l, dv), s_fin


def bidirectional_delta(lat, ctx):
    q, k, v, la, be = lat
    qc, kc, vc, lac, bec = ctx
    s0 = jnp.zeros((q.shape[0], H_B, DH_B, DH_B), f32)
    fl = lambda t: jnp.flip(t, axis=2)
    oc_f, s_f = gated_delta_chunked(qc, kc, vc, lac[:, 0], bec[:, 0], s0)
    oc_b, s_b = gated_delta_chunked(fl(qc), fl(kc), fl(vc), fl(lac[:, 1]), fl(bec[:, 1]), s0)
    o_f, _ = gated_delta_chunked(q, k, v, la[:, 0], be[:, 0], s_f)
    o_b, _ = gated_delta_chunked(fl(q), fl(k), fl(v), fl(la[:, 1]), fl(be[:, 1]), s_b)
    return o_f + fl(o_b), oc_f + fl(oc_b)


def gated_deltanet(qkv, gate, gates, qkv_c, gate_c, gates_c, conv_w, a_log, dt_bias, gain, ctx_out):
    lat = gdn_inputs(qkv, gates, conv_w, a_log, dt_bias)
    ctxi = gdn_inputs(qkv_c, gates_c, conv_w, a_log, dt_bias)
    o_lat, o_ctx = bidirectional_delta(lat, ctxi)

    def post(o, gg):
        b, l, _ = gg.shape
        o = rms_norm(o.transpose(0, 2, 1, 3), gain) * jax.nn.silu(gg.astype(f32).reshape(b, l, H_B, DH_B))
        return o.reshape(b, l, B_W).astype(gg.dtype)

    return post(o_lat, gate), (post(o_ctx, gate_c) if ctx_out else None)


def even_mixer(h, hc, w_in, w_out, qk_gain, lam_vec, subln, lam_init, conv_w, a_log, dt_bias,
               gdn_gain, cos, sin, ctx_out):
    qa, ka, va, qkv_b, g_b, gates_b = jnp.split(h @ w_in, EVEN_CUTS, axis=-1)
    qac, kac, vac, qkv_bc, g_bc, gates_bc = jnp.split(hc @ w_in, EVEN_CUTS, axis=-1)
    a_lat, a_ctx = diff_attention(qa, ka, va, qac, kac, vac, qk_gain, lam_vec, subln, lam_init,
                                  cos, sin, ctx_out)
    b_lat, b_ctx = gated_deltanet(qkv_b, g_b, gates_b, qkv_bc, g_bc, gates_bc, conv_w, a_log,
                                  dt_bias, gdn_gain, ctx_out)
    out = jnp.concatenate([a_lat, b_lat], axis=-1) @ w_out
    out_c = jnp.concatenate([a_ctx, b_ctx], axis=-1) @ w_out if ctx_out else None
    return out, out_c


def neighbourhood_attention(q, k, v, k_ctx, v_ctx, rpb):
    b, h, l, d = q.shape
    rows = l // GRID_W
    wr = min(WIN_R, rows)
    scale = d ** -0.5
    grid = lambda t: t.reshape(b, h, rows, GRID_W, d)
    qg, kg, vg = grid(q), grid(k), grid(v)
    cols = np.arange(GRID_W)
    c_start = np.clip(cols - WIN_C // 2, 0, GRID_W - WIN_C)
    col_mask = (cols[None, :] >= c_start[:, None]) & (cols[None, :] < c_start[:, None] + WIN_C)
    dc_idx = np.clip(cols[None, :] - cols[:, None] + WIN_C - 1, 0, 2 * WIN_C - 2)
    rpb_c = rpb[:, :, dc_idx]

    def row_block(r):
        r_start = jnp.clip(r - wr // 2, 0, rows - wr)
        q_r = lax.dynamic_index_in_dim(qg, r, axis=2, keepdims=False)
        k_b = lax.dynamic_slice_in_dim(kg, r_start, wr, axis=2)
        v_b = lax.dynamic_slice_in_dim(vg, r_start, wr, axis=2)
        dr_idx = r_start + jnp.arange(wr) - r + WIN_R - 1
        bias = jnp.take(rpb_c, dr_idx, axis=1).transpose(0, 2, 1, 3)
        s_lat = jnp.einsum('bhqd,bhrkd->bhqrk', q_r, k_b).astype(f32) * scale + bias.astype(f32)
        s_lat = jnp.where(col_mask[:, None, :], s_lat, -jnp.inf).reshape(b, h, GRID_W, wr * GRID_W)
        s_ctx = jnp.einsum('bhqd,bhkd->bhqk', q_r, k_ctx).astype(f32) * scale
        p = jax.nn.softmax(jnp.concatenate([s_lat, s_ctx], axis=-1), axis=-1).astype(v.dtype)
        p_lat = p[..., :wr * GRID_W].reshape(b, h, GRID_W, wr, GRID_W)
        return (jnp.einsum('bhqrk,bhrkd->bhqd', p_lat, v_b)
                + jnp.einsum('bhqk,bhkd->bhqd', p[..., wr * GRID_W:], v_ctx))

    out = lax.map(row_block, jnp.arange(rows))
    return jnp.moveaxis(out, 0, 2).reshape(b, h, l, d)


def odd_mixer(h, hc, w_in, w_out, qk_gain, rpb, ctx_out):
    q, k, v = jnp.split(h @ w_in, 3, axis=-1)
    kc, vc = jnp.split(hc @ w_in[:, MIX_ODD:], 2, axis=-1)
    q, k, v = rms_norm(heads(q, H_C), qk_gain[0]), rms_norm(heads(k, H_C), qk_gain[1]), heads(v, H_C)
    kc, vc = rms_norm(heads(kc, H_C), qk_gain[1]), heads(vc, H_C)
    out = merge_heads(neighbourhood_attention(q, k, v, kc, vc, rpb)) @ w_out
    out_c = None
    if ctx_out:
        qc = rms_norm(heads(hc @ w_in[:, :MIX_ODD], H_C), qk_gain[0])
        out_c = merge_heads(softmax_attention(qc, kc, vc)) @ w_out
    return out, out_c


def swiglu(h, w_in, w_out):
    gt, up = jnp.split(h @ w_in, 2, axis=-1)
    return (jax.nn.silu(gt) * up) @ w_out


def setup_inputs(seed: int = 0) -> dict:
    key = jax.random.key(seed)
    ks = jax.random.split(key, 24)
    d = D_MODEL
    nrm = lambda k, shape, s: jax.random.normal(k, shape, f32) * s
    dt = jnp.exp(jax.random.uniform(ks[17], (N_EVEN, 2, H_B), f32, math.log(1e-3), math.log(1e-1)))
    return {
        "x": nrm(ks[0], (BATCH, SEQ, d), 1.0),
        "c": nrm(ks[1], (BATCH, d), 1.0),
        "ctx": nrm(ks[2], (BATCH, CTX_LEN, d), 1.0),
        "c_ctx": nrm(ks[3], (d,), 1.0),
        "ada_w": nrm(ks[4], (DEPTH, d, 6 * d), 0.5 * d ** -0.5),
        "ada_b": nrm(ks[5], (DEPTH, 6 * d), 0.01),
        "norm_mix": 1.0 + nrm(ks[6], (DEPTH, d), 0.02),
        "norm_ffn": 1.0 + nrm(ks[7], (DEPTH, d), 0.02),
        "ffn_w_in": nrm(ks[8], (DEPTH, d, 2 * FF_HIDDEN), d ** -0.5),
        "ffn_w_out": nrm(ks[9], (DEPTH, FF_HIDDEN, d), FF_HIDDEN ** -0.5),
        "even_w_in": nrm(ks[10], (N_EVEN, d, EVEN_IN), d ** -0.5),
        "even_w_out": nrm(ks[11], (N_EVEN, MIX_EVEN, d), MIX_EVEN ** -0.5),
        "diff_qk_gain": 1.0 + nrm(ks[12], (N_EVEN, 2, DH_A), 0.02),
        "diff_lambda": nrm(ks[13], (N_EVEN, 4, DH_A), 0.1),
        "diff_subln": 1.0 + nrm(ks[14], (N_EVEN, DV_A), 0.02),
        "gdn_conv": nrm(ks[15], (N_EVEN, CONV_K, 3 * B_W), CONV_K ** -0.5),
        "gdn_a_log": jnp.log(jax.random.uniform(ks[16], (N_EVEN, 2, H_B), f32, 1.0, 16.0)),
        "gdn_dt_bias": dt + jnp.log(-jnp.expm1(-dt)),
        "gdn_norm": 1.0 + nrm(ks[18], (N_EVEN, DH_B), 0.02),
        "odd_w_in": nrm(ks[19], (N_ODD, d, 3 * MIX_ODD), d ** -0.5),
        "odd_w_out": nrm(ks[20], (N_ODD, MIX_ODD, d), MIX_ODD ** -0.5),
        "na_qk_gain": 1.0 + nrm(ks[21], (N_ODD, 2, DH_C), 0.02),
        "na_rpb": nrm(ks[22], (N_ODD, H_C, 2 * WIN_R - 1, 2 * WIN_C - 1), 0.02),
    }


def reference(x, c, ctx, c_ctx, ada_w, ada_b, norm_mix, norm_ffn, ffn_w_in, ffn_w_out,
              even_w_in, even_w_out, diff_qk_gain, diff_lambda, diff_subln, gdn_conv, gdn_a_log,
              gdn_dt_bias, gdn_norm, odd_w_in, odd_w_out, na_qk_gain, na_rpb):
    cos, sin = axial_rope(x.shape[1], DH_A)
    silu_c = jax.nn.silu(c)
    silu_cc = jax.nn.silu(c_ctx)
    for l in range(DEPTH):
        ctx_out = l < DEPTH - 1
        m = silu_c @ ada_w[l] + ada_b[l]
        mc = silu_cc @ ada_w[l] + ada_b[l]
        sh_m, sc_m, g_m, sh_f, sc_f, g_f = [t[:, None, :] for t in jnp.split(m, 6, axis=-1)]
        csh_m, csc_m, cg_m, csh_f, csc_f, cg_f = jnp.split(mc, 6)
        h = rms_norm(x, norm_mix[l]) * (1.0 + sc_m) + sh_m
        hc = rms_norm(ctx, norm_mix[l]) * (1.0 + csc_m) + csh_m
        if l % 2 == 0:
            e = l // 2
            lam_init = 0.8 - 0.6 * math.exp(-0.3 * l)
            o, oc = even_mixer(h, hc, even_w_in[e], even_w_out[e], diff_qk_gain[e], diff_lambda[e],
                               diff_subln[e], lam_init, gdn_conv[e], gdn_a_log[e], gdn_dt_bias[e],
                               gdn_norm[e], cos, sin, ctx_out)
        else:
            od = l // 2
            o, oc = odd_mixer(h, hc, odd_w_in[od], odd_w_out[od], na_qk_gain[od], na_rpb[od], ctx_out)
        x = x + g_m * o
        hf = rms_norm(x, norm_ffn[l]) * (1.0 + sc_f) + sh_f
        x = x + g_f * swiglu(hf, ffn_w_in[l], ffn_w_out[l])
        if ctx_out:
            ctx = ctx + cg_m * oc
            hcf = rms_norm(ctx, norm_ffn[l]) * (1.0 + csc_f) + csh_f
            ctx = ctx + cg_f * swiglu(hcf, ffn_w_in[l], ffn_w_out[l])
    return x
```

```python
import functools
import math

import numpy as np
import jax
import jax.numpy as jnp
from jax import lax
from jax.experimental import pallas as pl
from jax.experimental.pallas import tpu as pltpu

f32 = jnp.float32
bf16 = jnp.bfloat16

EPS = 1e-6
ROPE_BASE = 10000.0
GRID_W = 64
H_A = 4
DH_A = 64
DV_A = 2 * DH_A
A_W = H_A * DV_A
H_B = 4
DH_B = 128
B_W = H_B * DH_B
GDN_CHUNK = 64
CONV_K = 5
H_C = 8
DH_C = 128
WIN_R = 8
WIN_C = 16

LANES = 128
VMEM_LIMIT = 56 * 1024 * 1024
NEG = -1e30


def _cparams(sem):
    return pltpu.CompilerParams(dimension_semantics=sem, vmem_limit_bytes=VMEM_LIMIT)


def _silu(x):
    return x * (1.0 / (1.0 + jnp.exp(-x)))


def _dot(a, b):
    return jnp.dot(a, b, preferred_element_type=f32)


def _dot_nt(a, b):
    return lax.dot_general(a, b, (((1,), (1,)), ((), ())), preferred_element_type=f32)


def _split_bf16(x):
    hi = x.astype(bf16)
    lo = (x - hi.astype(f32)).astype(bf16)
    return hi, lo


def _dot3(a, b):
    ah, al = _split_bf16(a)
    bh, bl = _split_bf16(b)
    return _dot(ah, bh) + (_dot(ah, bl) + _dot(al, bh))


def _dot3_nt(a, b):
    ah, al = _split_bf16(a)
    bh, bl = _split_bf16(b)
    return _dot_nt(ah, bh) + (_dot_nt(ah, bl) + _dot_nt(al, bh))


def _ada_kernel(c_ref, w_ref, b_ref, o_ref):
    sc = _silu(c_ref[...])
    o_ref[0] = _dot3(sc, w_ref[0]) + b_ref[0]


def ada_modulation(c_all, ada_w, ada_b):
    depth, d, n = ada_w.shape
    rows = c_all.shape[0]
    tn = 1536
    return pl.pallas_call(
        _ada_kernel,
        out_shape=jax.ShapeDtypeStruct((depth, rows, n), f32),
        grid=(depth, n // tn),
        in_specs=[
            pl.BlockSpec((rows, d), lambda l, j: (0, 0)),
            pl.BlockSpec((1, d, tn), lambda l, j: (l, 0, j)),
            pl.BlockSpec((1, 1, tn), lambda l, j: (l, 0, j)),
        ],
        out_specs=pl.BlockSpec((1, rows, tn), lambda l, j: (l, 0, j)),
        compiler_params=_cparams(("parallel", "parallel")),
        name="ada_modulation",
    )(c_all, ada_w, ada_b.reshape(depth, 1, n))


def _norm_mod(x, gain, sc, sh):
    y = x * lax.rsqrt(jnp.mean(x * x, axis=-1, keepdims=True) + EPS)
    return (y * gain) * (1.0 + sc) + sh


def _swap_pairs(x):
    lane = lax.broadcasted_iota(jnp.int32, x.shape, x.ndim - 1)
    nxt = pltpu.roll(x, x.shape[-1] - 1, x.ndim - 1)
    prv = pltpu.roll(x, 1, x.ndim - 1)
    return jnp.where(lane % 2 == 0, nxt, prv)


def _submap_norm_rope(t, gain, cos, sin, scale):
    lane = lax.broadcasted_iota(jnp.int32, t.shape, 1)
    lo = lane < DH_A
    sq = t * t
    s_lo = jnp.sum(jnp.where(lo, sq, 0.0), axis=-1, keepdims=True)
    s_hi = jnp.sum(jnp.where(lo, 0.0, sq), axis=-1, keepdims=True)
    r = jnp.where(lo, lax.rsqrt(s_lo * (1.0 / DH_A) + EPS), lax.rsqrt(s_hi * (1.0 / DH_A) + EPS))
    y = t * r * gain
    if cos is not None:
        y = y * cos + _swap_pairs(y) * sin
    if scale != 1.0:
        y = y * scale
    return y


def _head_norm(t, gain, scale):
    y = t * lax.rsqrt(jnp.mean(t * t, axis=-1, keepdims=True) + EPS) * gain
    if scale != 1.0:
        y = y * scale
    return y


def _proj_kernel(*refs, epilogues, rope):
    n_out = len(epilogues)
    x_ref, gain_ref, sc_ref, sh_ref = refs[:4]
    pos = 4
    if rope:
        cos_ref, sin_ref = refs[pos:pos + 2]
        pos += 2
    qkg_ref = refs[pos]
    pos += 1
    w_refs = refs[pos:pos + n_out]
    o_refs = refs[pos + n_out:pos + 2 * n_out]

    h = _norm_mod(x_ref[0], gain_ref[...], sc_ref[0], sh_ref[0]).astype(bf16)
    for w_ref, o_ref, epi in zip(w_refs, o_refs, epilogues):
        acc = _dot(h, w_ref[...])
        kind = epi[0]
        if kind == "plain":
            o_ref[0] = acc.astype(o_ref.dtype)
        elif kind == "submap":
            _, grow, use_rope, scale = epi
            gain = qkg_ref[grow:grow + 1, :]
            for hd in range(acc.shape[1] // LANES):
                t = acc[:, hd * LANES:(hd + 1) * LANES]
                cs = (cos_ref[...], sin_ref[...]) if (rope and use_rope) else (None, None)
                y = _submap_norm_rope(t, gain, cs[0], cs[1], scale)
                o_ref[0, :, hd * LANES:(hd + 1) * LANES] = y.astype(o_ref.dtype)
        elif kind == "headnorm":
            _, grow, scale = epi
            gain = qkg_ref[grow:grow + 1, :]
            for hd in range(acc.shape[1] // LANES):
                t = acc[:, hd * LANES:(hd + 1) * LANES]
                o_ref[0, :, hd * LANES:(hd + 1) * LANES] = _head_norm(t, gain, scale).astype(o_ref.dtype)
        else:
            raise ValueError(kind)


def norm_mod_project(x, gain, sc, sh, weights, out_dtypes, epilogues, qk_gain, rope_tabs=None,
                     tm=512, name="proj"):
    b, l, d = x.shape
    tm = min(tm, l)
    assert l % tm == 0
    bm = sc.shape[0]
    mod_map = (lambda bi, i: (bi, 0, 0)) if bm == b else (lambda bi, i: (0, 0, 0))
    rope = rope_tabs is not None
    in_specs = [
        pl.BlockSpec((1, tm, d), lambda bi, i: (bi, i, 0)),
        pl.BlockSpec((1, d), lambda bi, i: (0, 0)),
        pl.BlockSpec((1, 1, d), mod_map),
        pl.BlockSpec((1, 1, d), mod_map),
    ]
    args = [x, gain, sc, sh]
    if rope:
        in_specs += [pl.BlockSpec((tm, LANES), lambda bi, i: (i, 0))] * 2
        args += list(rope_tabs)
    in_specs.append(pl.BlockSpec(qk_gain.shape, lambda bi, i: (0, 0)))
    args.append(qk_gain)
    for w in weights:
        in_specs.append(pl.BlockSpec(w.shape, lambda bi, i: (0, 0)))
        args.append(w)
    out_shape = [jax.ShapeDtypeStruct((b, l, w.shape[1]), dt) for w, dt in zip(weights, out_dtypes)]
    out_specs = [pl.BlockSpec((1, tm, w.shape[1]), lambda bi, i: (bi, i, 0)) for w in weights]
    return pl.pallas_call(
        functools.partial(_proj_kernel, epilogues=tuple(epilogues), rope=rope),
        out_shape=out_shape,
        grid=(b, l // tm),
        in_specs=in_specs,
        out_specs=out_specs,
        compiler_params=_cparams(("parallel", "parallel")),
        name=name,
    )(*args)


def _diff_attn_kernel(*refs, has_lat, tk, lam_init):
    if has_lat:
        lam_ref, q_ref, kc_ref, vc_ref, k_ref, v_ref, subln_ref, o_ref = refs
    else:
        lam_ref, q_ref, kc_ref, vc_ref, subln_ref, o_ref = refs
    q = q_ref[0]
    tq = q.shape[0]
    lane = lax.broadcasted_iota(jnp.int32, q.shape, 1)
    zero = jnp.zeros_like(q)
    q1 = jnp.where(lane < DH_A, q, zero)
    q2 = jnp.where(lane < DH_A, zero, q)

    def one_map(qm, kblk, vblk, m, l, acc):
        s = _dot_nt(qm, kblk)
        m_new = jnp.maximum(m, jnp.max(s, axis=-1, keepdims=True))
        p = jnp.exp(s - m_new)
        a = jnp.exp(m - m_new)
        l = a * l + jnp.sum(p, axis=-1, keepdims=True)
        acc = a * acc + _dot(p.astype(bf16), vblk)
        return m_new, l, acc

    def step(kblk, vblk, carry):
        m1, l1, a1, m2, l2, a2 = carry
        m1, l1, a1 = one_map(q1, kblk, vblk, m1, l1, a1)
        m2, l2, a2 = one_map(q2, kblk, vblk, m2, l2, a2)
        return m1, l1, a1, m2, l2, a2

    mi = jnp.full((tq, 1), NEG, f32)
    li = jnp.zeros((tq, 1), f32)
    ai = jnp.zeros((tq, DV_A), f32)
    carry = step(kc_ref[0], vc_ref[0], (mi, li, ai, mi, li, ai))
    if has_lat:
        n_blk = k_ref.shape[1] // tk

        def body(j, c):
            off = pl.multiple_of(j * tk, tk)
            return step(k_ref[0, pl.ds(off, tk), :], v_ref[0, pl.ds(off, tk), :], c)

        carry = lax.fori_loop(0, n_blk, body, carry)
    m1, l1, a1, m2, l2, a2 = carry
    lam = lam_ref[0]
    o = a1 * (1.0 / l1) - lam * (a2 * (1.0 / l2))
    y = o * lax.rsqrt(jnp.mean(o * o, axis=-1, keepdims=True) + EPS) * subln_ref[...]
    o_ref[0] = (y * (1.0 - lam_init)).astype(o_ref.dtype)


def diff_attention(lam, q, k_ctx, v_ctx, k_lat, v_lat, subln, lam_init, tq=512, tk=512, name="diff_attn"):
    b, lq, _ = q.shape
    lc = k_ctx.shape[1]
    has_lat = k_lat is not None
    tq = min(tq, lq)
    assert lq % tq == 0
    head_blk = lambda rows: pl.BlockSpec((1, rows, LANES), lambda bi, h, i: (bi, 0, h))
    in_specs = [
        pl.BlockSpec(memory_space=pltpu.SMEM),
        pl.BlockSpec((1, tq, LANES), lambda bi, h, i: (bi, i, h)),
        head_blk(lc), head_blk(lc),
    ]
    args = [lam, q, k_ctx, v_ctx]
    if has_lat:
        ll = k_lat.shape[1]
        tk = min(tk, ll)
        assert ll % tk == 0
        in_specs += [head_blk(ll), head_blk(ll)]
        args += [k_lat, v_lat]
    in_specs.append(pl.BlockSpec((1, LANES), lambda bi, h, i: (0, 0)))
    args.append(subln)
    return pl.pallas_call(
        functools.partial(_diff_attn_kernel, has_lat=has_lat, tk=tk, lam_init=lam_init),
        out_shape=jax.ShapeDtypeStruct((b, lq, A_W), bf16),
        grid=(b, H_A, lq // tq),
        in_specs=in_specs,
        out_specs=pl.BlockSpec((1, tq, LANES), lambda bi, h, i: (bi, i, h)),
        compiler_params=_cparams(("parallel", "parallel", "arbitrary")),
        name=name,
    )(*args)


def _gdn_conv_kernel(x_ref, w_ref, o_ref, pad_ref, *, rows_per_step):
    l = x_ref.shape[1]
    half = CONV_K // 2
    halo = 8
    zeros = jnp.zeros((halo, LANES), f32)
    pad_ref[0:halo, :] = zeros
    pad_ref[halo + l:halo + l + halo, :] = zeros
    pad_ref[halo:halo + l, :] = x_ref[0]
    kind = pl.program_id(1) // H_B
    w = w_ref[...]
    r = rows_per_step

    def body(i, _):
        t0 = pl.multiple_of(i * r, r)
        win = pad_ref[pl.ds(t0, r + 2 * halo), :]
        y = jnp.zeros((r, LANES), f32)
        for j in range(CONV_K):
            s = halo - half + j
            y = y + win[s:s + r, :] * w[j:j + 1, :]
        y = _silu(y)
        nrm = lax.rsqrt(jnp.sum(y * y, axis=-1, keepdims=True) + EPS)
        nrm = jnp.where(kind == 0, nrm * (DH_B ** -0.5), nrm)
        y = jnp.where(kind == 2, y, y * nrm)
        o_ref[0, pl.ds(t0, r), :] = y
        return 0

    lax.fori_loop(0, l // r, body, 0)


def gdn_short_conv(qkv, conv_w):
    b, l, c = qkv.shape
    r = min(512, l)
    assert l % r == 0
    wpad = jnp.zeros((8, c), f32).at[:CONV_K].set(conv_w)
    return pl.pallas_call(
        functools.partial(_gdn_conv_kernel, rows_per_step=r),
        out_shape=jax.ShapeDtypeStruct((b, l, c), f32),
        grid=(b, c // LANES),
        in_specs=[
            pl.BlockSpec((1, l, LANES), lambda bi, j: (bi, 0, j)),
            pl.BlockSpec((8, LANES), lambda bi, j: (0, j)),
        ],
        out_specs=pl.BlockSpec((1, l, LANES), lambda bi, j: (bi, 0, j)),
        scratch_shapes=[pltpu.VMEM((l + 16, LANES), f32)],
        compiler_params=_cparams(("parallel", "parallel")),
        name="gdn_conv",
    )(qkv, wpad)


def _softplus(x):
    return jnp.maximum(x, 0.0) + jnp.log1p(jnp.exp(-jnp.abs(x)))


def _sigmoid(x):
    return 1.0 / (1.0 + jnp.exp(-x))


def _unit_tri_inverse(lmat):
    n = lmat.shape[0]
    ci = lax.broadcasted_iota(jnp.int32, (n, n), 0)
    si = lax.broadcasted_iota(jnp.int32, (n, n), 1)
    p = jnp.where(ci == si, 1.0, 0.0) - lmat
    m = lmat
    k = 2
    while k < n:
        m = _dot3(m, m)
        p = p + _dot3(p, m)
        k *= 2
    return p


def _gdn_kernel(xf_ref, xb_ref, gcf_ref, gcb_ref, grf_ref, grb_ref, pcol_ref, prow_ref, s0_ref,
                of_ref, ob_ref, s_ref):
    i = pl.program_id(1)

    @pl.when(i == 0)
    def _():
        s_ref[...] = s0_ref[...]

    c = GDN_CHUNK
    ci = lax.broadcasted_iota(jnp.int32, (c, c), 0)
    si = lax.broadcasted_iota(jnp.int32, (c, c), 1)
    for d, (x_ref, gc_ref, gr_ref, o_ref) in enumerate(
            ((xf_ref, gcf_ref, grf_ref, of_ref), (xb_ref, gcb_ref, grb_ref, ob_ref))):
        incl = (ci >= si) if d == 0 else (ci <= si)
        incl_t = (ci <= si) if d == 0 else (ci >= si)
        strict = (ci > si) if d == 0 else (ci < si)
        gcol = gc_ref[0]
        grow = gr_ref[0, 0]
        beta_all = _sigmoid(gcol)
        la_col_all = -jnp.exp(pcol_ref[0:1, :]) * _softplus(gcol + pcol_ref[1:2, :])
        la_row_all = -jnp.exp(prow_ref[:, 0:1]) * _softplus(grow + prow_ref[:, 1:2])
        for h in range(H_B):
            q = x_ref[0, :, h * DH_B:(h + 1) * DH_B]
            k = x_ref[0, :, B_W + h * DH_B:B_W + (h + 1) * DH_B]
            v = x_ref[0, :, 2 * B_W + h * DH_B:2 * B_W + (h + 1) * DH_B]
            gi = d * H_B + h
            beta = beta_all[:, gi:gi + 1]
            la_col = la_col_all[:, 2 * H_B + gi:2 * H_B + gi + 1]
            la_row = la_row_all[2 * H_B + gi:2 * H_B + gi + 1, :]
            g_col = jnp.sum(jnp.where(incl, la_row, 0.0), axis=1, keepdims=True)
            g_row = jnp.sum(jnp.where(incl_t, la_col, 0.0), axis=0, keepdims=True)
            g_tot = jnp.sum(la_row, axis=1, keepdims=True)
            decay = jnp.where(incl, jnp.exp(jnp.where(incl, g_col - g_row, 0.0)), 0.0)
            kk = _dot3_nt(k, k)
            lmat = jnp.where(strict, beta * decay * kk, 0.0)
            eg = jnp.exp(g_col)
            rhs = jnp.concatenate([beta * v, (beta * eg) * k], axis=-1)
            sol = _dot3(_unit_tri_inverse(lmat), rhs)
            u0, w = sol[:, :DH_B], sol[:, DH_B:]
            aqk = _dot3_nt(q, k) * decay
            qg = q * eg
            kg = k * jnp.exp(g_tot - g_col)
            s = s_ref[0, d, h]
            u = u0 - _dot3(w, s)
            o = _dot3(qg, s) + _dot3(aqk, u)
            s_ref[0, d, h] = jnp.exp(g_tot) * s + _dot3(kg.T, u)
            o_ref[0, :, h * DH_B:(h + 1) * DH_B] = o


def gdn_scan(x, gates_col, gates_row, pcol, prow, s0):
    b, l, _ = x.shape
    c = GDN_CHUNK
    nc = l // c
    fwd = lambda bi, i: (bi, i, 0)
    bwd = lambda bi, i: (bi, nc - 1 - i, 0)
    st_spec = pl.BlockSpec((1, 2, H_B, DH_B, DH_B), lambda bi, i: (bi, 0, 0, 0, 0))
    return pl.pallas_call(
        _gdn_kernel,
        out_shape=[jax.ShapeDtypeStruct((b, l, B_W), f32), jax.ShapeDtypeStruct((b, l, B_W), f32),
                   jax.ShapeDtypeStruct(s0.shape, f32)],
        grid=(b, nc),
        in_specs=[
            pl.BlockSpec((1, c, 3 * B_W), fwd), pl.BlockSpec((1, c, 3 * B_W), bwd),
            pl.BlockSpec((1, c, LANES), fwd), pl.BlockSpec((1, c, LANES), bwd),
            pl.BlockSpec((1, 1, 4 * H_B, c), lambda bi, i: (bi, i, 0, 0)),
            pl.BlockSpec((1, 1, 4 * H_B, c), lambda bi, i: (bi, nc - 1 - i, 0, 0)),
            pl.BlockSpec((8, LANES), lambda bi, i: (0, 0)),
            pl.BlockSpec((4 * H_B, LANES), lambda bi, i: (0, 0)),
            st_spec,
        ],
        out_specs=[pl.BlockSpec((1, c, B_W), fwd), pl.BlockSpec((1, c, B_W), bwd), st_spec],
        compiler_params=_cparams(("parallel", "arbitrary")),
        name="gdn_scan",
    )(x, x, gates_col, gates_col, gates_row, gates_row, pcol, prow, s0)


def _na_kernel(q_ref, k_ref, v_ref, kc_ref, vc_ref, bias_ref, o_ref, *, rb, band, rows):
    i = pl.program_id(2)
    ub = jnp.clip(i * rb - WIN_R // 2, 0, rows - band)
    off = pl.multiple_of(ub * GRID_W, GRID_W)
    q = q_ref[0]
    kb = k_ref[0, pl.ds(off, band * GRID_W), :]
    vb = v_ref[0, pl.ds(off, band * GRID_W), :]
    s_lat = _dot_nt(q, kb) + bias_ref[0, 0]
    s_ctx = _dot_nt(q, kc_ref[0])
    m = jnp.maximum(jnp.max(s_lat, axis=-1, keepdims=True), jnp.max(s_ctx, axis=-1, keepdims=True))
    p_lat = jnp.exp(s_lat - m)
    p_ctx = jnp.exp(s_ctx - m)
    den = jnp.sum(p_lat, axis=-1, keepdims=True) + jnp.sum(p_ctx, axis=-1, keepdims=True)
    o = _dot(p_lat.astype(bf16), vb) + _dot(p_ctx.astype(bf16), vc_ref[0])
    o_ref[0] = (o * (1.0 / den)).astype(o_ref.dtype)


def _na_bias_index(rows, rb, band):
    wr = min(WIN_R, rows)
    nblk = rows // rb
    cols = np.arange(GRID_W)
    c_start = np.clip(cols - WIN_C // 2, 0, GRID_W - WIN_C)
    col_ok = (cols[None, :] >= c_start[:, None]) & (cols[None, :] < c_start[:, None] + WIN_C)
    dc_idx = np.clip(cols[None, :] - cols[:, None] + WIN_C - 1, 0, 2 * WIN_C - 2)

    def geometry(i):
        ub = int(np.clip(i * rb - WIN_R // 2, 0, rows - band))
        qr = i * rb + np.arange(rb)
        kr = ub + np.arange(band)
        r_start = np.clip(qr - wr // 2, 0, rows - wr)
        row_ok = (kr[None, :] >= r_start[:, None]) & (kr[None, :] < r_start[:, None] + wr)
        dr = np.clip(kr[None, :] - qr[:, None] + WIN_R - 1, 0, 2 * WIN_R - 2)
        ok = row_ok[:, None, :, None] & col_ok[None, :, None, :]
        dr_f = np.broadcast_to(dr[:, None, :, None], ok.shape)
        dc_f = np.broadcast_to(dc_idx[None, :, None, :], ok.shape)
        shp = (rb * GRID_W, band * GRID_W)
        return ok.reshape(shp), dr_f.reshape(shp), dc_f.reshape(shp)

    reps = [0, min(1, nblk - 1), nblk - 1]
    geo = [geometry(i) for i in reps]
    for i in range(1, nblk - 1):
        g = geometry(i)
        assert all(np.array_equal(a, b_) for a, b_ in zip(g, geo[1]))
    ok = np.stack([g[0] for g in geo])
    dr = np.stack([g[1] for g in geo])
    dc = np.stack([g[2] for g in geo])
    return ok, dr, dc


def neighbourhood_attention(q, k, v, k_ctx, v_ctx, rpb, rb=4):
    b, l, _ = q.shape
    lc = k_ctx.shape[1]
    rows = l // GRID_W
    band = rb + WIN_R - 1
    assert rows % rb == 0 and rows >= band and WIN_R <= rows
    nblk = rows // rb
    ok, dr, dc = _na_bias_index(rows, rb, band)
    bias = jnp.where(ok[:, None], rpb[:, dr, dc].transpose(1, 0, 2, 3), NEG).astype(f32)
    tq = rb * GRID_W
    variant = lambda i: jnp.where(i == 0, 0, jnp.where(i == nblk - 1, 2, 1))
    head_blk = lambda n: pl.BlockSpec((1, n, LANES), lambda bi, h, i: (bi, 0, h))
    return pl.pallas_call(
        functools.partial(_na_kernel, rb=rb, band=band, rows=rows),
        out_shape=jax.ShapeDtypeStruct((b, l, H_C * DH_C), bf16),
        grid=(b, H_C, nblk),
        in_specs=[
            pl.BlockSpec((1, tq, LANES), lambda bi, h, i: (bi, i, h)),
            head_blk(l), head_blk(l), head_blk(lc), head_blk(lc),
            pl.BlockSpec((1, 1, tq, band * GRID_W), lambda bi, h, i: (variant(i), h, 0, 0)),
        ],
        out_specs=pl.BlockSpec((1, tq, LANES), lambda bi, h, i: (bi, i, h)),
        compiler_params=_cparams(("parallel", "parallel", "arbitrary")),
        name="na_attn",
    )(q, k, v, k_ctx, v_ctx, bias)


def _outproj_kernel(*refs, has_gdn):
    if has_gdn:
        x_ref, g_ref, a_ref, wa_ref, of_ref, ob_ref, gate_ref, gain_ref, wb_ref, o_ref = refs
    else:
        x_ref, g_ref, a_ref, wa_ref, o_ref = refs
    acc = _dot(a_ref[0], wa_ref[...])
    if has_gdn:
        o = of_ref[0] + ob_ref[0]
        gate = gate_ref[0]
        parts = []
        for h in range(H_B):
            sl = slice(h * DH_B, (h + 1) * DH_B)
            t = o[:, sl]
            y = t * lax.rsqrt(jnp.mean(t * t, axis=-1, keepdims=True) + EPS) * gain_ref[...]
            parts.append((y * _silu(gate[:, sl])).astype(bf16))
        acc = acc + _dot(jnp.concatenate(parts, axis=-1), wb_ref[...])
    o_ref[0] = x_ref[0] + g_ref[0] * acc


def out_project(x, g, a, wa, gdn=None, tm=512, name="outproj"):
    b, l, d = x.shape
    tm = min(tm, l)
    assert l % tm == 0
    bm = g.shape[0]
    mod_map = (lambda bi, i: (bi, 0, 0)) if bm == b else (lambda bi, i: (0, 0, 0))
    row = lambda n: pl.BlockSpec((1, tm, n), lambda bi, i: (bi, i, 0))
    full = lambda arr: pl.BlockSpec(arr.shape, lambda bi, i: (0, 0))
    in_specs = [row(d), pl.BlockSpec((1, 1, d), mod_map), row(a.shape[2]), full(wa)]
    args = [x, g, a, wa]
    if gdn is not None:
        o_f, o_b, gate, gain, wb = gdn
        in_specs += [row(B_W), row(B_W), row(B_W), full(gain), full(wb)]
        args += [o_f, o_b, gate, gain, wb]
    return pl.pallas_call(
        functools.partial(_outproj_kernel, has_gdn=gdn is not None),
        out_shape=jax.ShapeDtypeStruct((b, l, d), f32),
        grid=(b, l // tm),
        in_specs=in_specs,
        out_specs=row(d),
        compiler_params=_cparams(("parallel", "parallel")),
        name=name,
    )(*args)


def _ffn_kernel(x_ref, gain_ref, sc_ref, sh_ref, g_ref, wg_ref, wu_ref, wo_ref, o_ref, h_ref, acc_ref):
    j = pl.program_id(2)

    @pl.when(j == 0)
    def _():
        h_ref[...] = _norm_mod(x_ref[0], gain_ref[...], sc_ref[0], sh_ref[0]).astype(bf16)
        acc_ref[...] = jnp.zeros_like(acc_ref)

    h = h_ref[...]
    gt = _dot(h, wg_ref[...])
    up = _dot(h, wu_ref[...])
    acc_ref[...] += _dot((_silu(gt) * up).astype(bf16), wo_ref[...])

    @pl.when(j == pl.num_programs(2) - 1)
    def _():
        o_ref[0] = x_ref[0] + g_ref[0] * acc_ref[...]


def ffn(x, gain, sc, sh, g, w_in, w_out, tm=512, tf=256, name="ffn"):
    b, l, d = x.shape
    ff = w_out.shape[0]
    tm = min(tm, l)
    assert l % tm == 0 and ff % tf == 0
    nf = ff // tf
    bm = sc.shape[0]
    mod_map = (lambda bi, i, j: (bi, 0, 0)) if bm == b else (lambda bi, i, j: (0, 0, 0))
    mod = pl.BlockSpec((1, 1, d), mod_map)
    return pl.pallas_call(
        _ffn_kernel,
        out_shape=jax.ShapeDtypeStruct((b, l, d), f32),
        grid=(b, l // tm, nf),
        in_specs=[
            pl.BlockSpec((1, tm, d), lambda bi, i, j: (bi, i, 0)),
            pl.BlockSpec((1, d), lambda bi, i, j: (0, 0)),
            mod, mod, mod,
            pl.BlockSpec((d, tf), lambda bi, i, j: (0, j)),
            pl.BlockSpec((d, tf), lambda bi, i, j: (0, j + nf)),
            pl.BlockSpec((tf, d), lambda bi, i, j: (j, 0)),
        ],
        out_specs=pl.BlockSpec((1, tm, d), lambda bi, i, j: (bi, i, 0)),
        scratch_shapes=[pltpu.VMEM((tm, d), bf16), pltpu.VMEM((tm, d), f32)],
        compiler_params=_cparams(("parallel", "parallel", "arbitrary")),
        name=name,
    )(x, gain, sc, sh, g, w_in, w_in, w_out)


def _rope_tables(l):
    n_freq = DH_A // 4
    inv = ROPE_BASE ** (-jnp.arange(n_freq, dtype=f32) / n_freq)
    t = jnp.arange(l)
    row = (t // GRID_W).astype(f32)
    col = (t % GRID_W).astype(f32)
    ang = jnp.concatenate([row[:, None] * inv, col[:, None] * inv], axis=-1)
    cos = jnp.repeat(jnp.cos(ang), 2, axis=-1)
    sin = jnp.repeat(jnp.sin(ang), 2, axis=-1)
    sign = jnp.tile(jnp.array([-1.0, 1.0], f32), DH_A // 2)
    return jnp.tile(cos, (1, 2)), jnp.tile(sin * sign, (1, 2))


def _pad_rows(a, rows):
    return jnp.zeros((rows,) + a.shape[1:], a.dtype).at[:a.shape[0]].set(a)


def kernel(x, c, ctx, c_ctx, ada_w, ada_b, norm_mix, norm_ffn, ffn_w_in, ffn_w_out, even_w_in, even_w_out,
           diff_qk_gain, diff_lambda, diff_subln, gdn_conv, gdn_a_log, gdn_dt_bias, gdn_norm, odd_w_in,
           odd_w_out, na_qk_gain, na_rpb):
    b, l, d = x.shape
    lc = ctx.shape[1]
    depth = ada_w.shape[0]
    c_all = _pad_rows(jnp.concatenate([c, c_ctx[None]], axis=0), 16)
    mods = ada_modulation(c_all, ada_w, ada_b)
    cos_t, sin_t = _rope_tables(l)
    ctx_flat = ctx.reshape(1, b * lc, d)

    for layer in range(depth):
        ctx_out = layer < depth - 1
        m_lat = [t[:, None, :] for t in jnp.split(mods[layer, :b], 6, axis=-1)]
        m_ctx = [t[:, None, :] for t in jnp.split(mods[layer, b:b + 1], 6, axis=-1)]
        sh_m, sc_m, g_m, sh_f, sc_f, g_f = m_lat
        csh_m, csc_m, cg_m, csh_f, csc_f, cg_f = m_ctx
        gain_m = norm_mix[layer][None]
        gain_f = norm_ffn[layer][None]
        if layer % 2 == 0:
            e = layer // 2
            lam_init = 0.8 - 0.6 * math.exp(-0.3 * layer)
            w_in = even_w_in[e].astype(bf16)
            cuts = [0, A_W, 2 * A_W, 3 * A_W, 3 * A_W + 3 * B_W, 3 * A_W + 4 * B_W]
            ws = [w_in[:, cuts[i]:cuts[i + 1]] for i in range(5)]
            w_gates = jnp.zeros((d, LANES), bf16).at[:, :4 * H_B].set(w_in[:, cuts[5]:])
            ws.append(w_gates)
            qk_gain = _pad_rows(jnp.tile(diff_qk_gain[e], (1, 2)), 8)
            dts = [bf16, bf16, bf16, f32, f32, f32]
            epi = lambda rope: [("submap", 0, rope, DH_A ** -0.5), ("submap", 1, rope, 1.0), ("plain",),
                                ("plain",), ("plain",), ("plain",)]
            qa, ka, va, qkv_b, g_b, gates = norm_mod_project(
                x, gain_m, sc_m, sh_m, ws, dts, epi(True), qk_gain, rope_tabs=(cos_t, sin_t), name="even_proj")
            qac, kac, vac, qkv_bc, g_bc, gates_c = norm_mod_project(
                ctx_flat, gain_m, csc_m, csh_m, ws, dts, epi(False), qk_gain, name="even_proj_ctx")
            unflat = lambda t: t.reshape(b, lc, t.shape[-1])
            qac, kac, vac, qkv_bc, g_bc, gates_c = map(unflat, (qac, kac, vac, qkv_bc, g_bc, gates_c))

            lv = diff_lambda[e]
            lam = (jnp.exp(jnp.sum(lv[0] * lv[1])) - jnp.exp(jnp.sum(lv[2] * lv[3])) + lam_init).reshape(1)
            subln = diff_subln[e][None]
            a_lat = diff_attention(lam, qa, kac, vac, ka, va, subln, lam_init, name="diff_attn")
            a_ctx = diff_attention(lam, qac, kac, vac, None, None, subln, lam_init, name="diff_attn_ctx")

            y_lat = gdn_short_conv(qkv_b, gdn_conv[e])
            y_ctx = gdn_short_conv(qkv_bc, gdn_conv[e])
            pvec = jnp.concatenate([jnp.zeros((2, 2 * H_B), f32),
                                    jnp.stack([gdn_a_log[e].reshape(-1), gdn_dt_bias[e].reshape(-1)])], axis=1)
            pcol = jnp.zeros((8, LANES), f32).at[:2, :4 * H_B].set(pvec)
            prow = jnp.zeros((4 * H_B, LANES), f32).at[:, :2].set(pvec.T)
            to_rows = lambda g: g[:, :, :4 * H_B].reshape(g.shape[0], -1, GDN_CHUNK, 4 * H_B).transpose(0, 1, 3, 2)
            s0 = jnp.zeros((b, 2, H_B, DH_B, DH_B), f32)
            ocf, ocb, s_mid = gdn_scan(y_ctx, gates_c, to_rows(gates_c), pcol, prow, s0)
            olf, olb, _ = gdn_scan(y_lat, gates, to_rows(gates), pcol, prow, s_mid)

            w_out = even_w_out[e].astype(bf16)
            gdn_gain = gdn_norm[e][None]
            x = out_project(x, g_m, a_lat, w_out[:A_W], gdn=(olf, olb, g_b, gdn_gain, w_out[A_W:]),
                            name="even_out")
            ctx_flat = out_project(
                ctx_flat, cg_m, a_ctx.reshape(1, b * lc, A_W), w_out[:A_W],
                gdn=(ocf.reshape(1, b * lc, B_W), ocb.reshape(1, b * lc, B_W), g_bc.reshape(1, b * lc, B_W),
                     gdn_gain, w_out[A_W:]), name="even_out_ctx")
        else:
            od = layer // 2
            w_in = odd_w_in[od].astype(bf16)
            mix = H_C * DH_C
            ws = [w_in[:, :mix], w_in[:, mix:2 * mix], w_in[:, 2 * mix:]]
            qk_gain = _pad_rows(na_qk_gain[od], 8)
            q, k, v = norm_mod_project(
                x, gain_m, sc_m, sh_m, ws, [bf16] * 3,
                [("headnorm", 0, DH_C ** -0.5), ("headnorm", 1, 1.0), ("plain",)], qk_gain, name="odd_proj")
            kc, vc = norm_mod_project(
                ctx_flat, gain_m, csc_m, csh_m, ws[1:], [bf16] * 2,
                [("headnorm", 1, 1.0), ("plain",)], qk_gain, name="odd_proj_ctx")
            kc, vc = kc.reshape(b, lc, mix), vc.reshape(b, lc, mix)
            o = neighbourhood_attention(q, k, v, kc, vc, na_rpb[od])
            x = out_project(x, g_m, o, odd_w_out[od].astype(bf16), name="odd_out")
            if ctx_out:
                raise NotImplementedError("context output of a neighbourhood layer is not needed at depth 2")

        wf_in = ffn_w_in[layer].astype(bf16)
        wf_out = ffn_w_out[layer].astype(bf16)
        x = ffn(x, gain_f, sc_f, sh_f, g_f, wf_in, wf_out, name="ffn")
        if ctx_out:
            ctx_flat = ffn(ctx_flat, gain_f, csc_f, csh_f, cg_f, wf_in, wf_out, name="ffn_ctx")
    return x
```

```python
import functools
import math

import numpy as np
import jax
import jax.numpy as jnp
from jax import lax
from jax.experimental import pallas as pl
from jax.experimental.pallas import tpu as pltpu

f32 = jnp.float32
bf16 = jnp.bfloat16

EPS = 1e-6
ROPE_BASE = 10000.0
GRID_W = 64
H_A = 4
DH_A = 64
DV_A = 2 * DH_A
A_W = H_A * DV_A
H_B = 4
DH_B = 128
B_W = H_B * DH_B
GDN_CHUNK = 64
CONV_K = 5
H_C = 8
DH_C = 128
WIN_R = 8
WIN_C = 16

LANES = 128
VMEM_LIMIT = 56 * 1024 * 1024
NEG = -1e30


def _cparams(sem):
    return pltpu.CompilerParams(dimension_semantics=sem, vmem_limit_bytes=VMEM_LIMIT)


def _silu(x):
    return x * (1.0 / (1.0 + jnp.exp(-x)))


def _dot(a, b):
    return jnp.dot(a, b, preferred_element_type=f32)


def _dot_nt(a, b):
    return lax.dot_general(a, b, (((1,), (1,)), ((), ())), preferred_element_type=f32)


def _split_bf16(x):
    hi = x.astype(bf16)
    lo = (x - hi.astype(f32)).astype(bf16)
    return hi, lo


def _dot3(a, b):
    ah, al = _split_bf16(a)
    bh, bl = _split_bf16(b)
    return _dot(ah, bh) + (_dot(ah, bl) + _dot(al, bh))


def _mm(a, b):
    return _dot(a.astype(bf16), b.astype(bf16))


def _dot3_nt(a, b):
    ah, al = _split_bf16(a)
    bh, bl = _split_bf16(b)
    return _dot_nt(ah, bh) + (_dot_nt(ah, bl) + _dot_nt(al, bh))


def _ada_kernel(c_ref, w_ref, b_ref, o_ref):
    sc = _silu(c_ref[...])
    o_ref[0] = _dot3(sc, w_ref[0]) + b_ref[0]


def ada_modulation(c_all, ada_w, ada_b):
    depth, d, n = ada_w.shape
    rows = c_all.shape[0]
    tn = 1536
    return pl.pallas_call(
        _ada_kernel,
        out_shape=jax.ShapeDtypeStruct((depth, rows, n), f32),
        grid=(depth, n // tn),
        in_specs=[
            pl.BlockSpec((rows, d), lambda l, j: (0, 0)),
            pl.BlockSpec((1, d, tn), lambda l, j: (l, 0, j)),
            pl.BlockSpec((1, 1, tn), lambda l, j: (l, 0, j)),
        ],
        out_specs=pl.BlockSpec((1, rows, tn), lambda l, j: (l, 0, j)),
        compiler_params=_cparams(("parallel", "parallel")),
        name="ada_modulation",
    )(c_all, ada_w, ada_b.reshape(depth, 1, n))


def _norm_mod(x, gain, sc, sh):
    y = x * lax.rsqrt(jnp.mean(x * x, axis=-1, keepdims=True) + EPS)
    return (y * gain) * (1.0 + sc) + sh


def _swap_pairs(x):
    lane = lax.broadcasted_iota(jnp.int32, x.shape, x.ndim - 1)
    nxt = pltpu.roll(x, x.shape[-1] - 1, x.ndim - 1)
    prv = pltpu.roll(x, 1, x.ndim - 1)
    return jnp.where(lane % 2 == 0, nxt, prv)


def _submap_norm_rope(t, gain, cos, sin, scale):
    lane = lax.broadcasted_iota(jnp.int32, t.shape, 1)
    lo = lane < DH_A
    sq = t * t
    s_lo = jnp.sum(jnp.where(lo, sq, 0.0), axis=-1, keepdims=True)
    s_hi = jnp.sum(jnp.where(lo, 0.0, sq), axis=-1, keepdims=True)
    r = jnp.where(lo, lax.rsqrt(s_lo * (1.0 / DH_A) + EPS), lax.rsqrt(s_hi * (1.0 / DH_A) + EPS))
    y = t * r * gain
    if cos is not None:
        y = y * cos + _swap_pairs(y) * sin
    if scale != 1.0:
        y = y * scale
    return y


def _head_norm(t, gain, scale):
    y = t * lax.rsqrt(jnp.mean(t * t, axis=-1, keepdims=True) + EPS) * gain
    if scale != 1.0:
        y = y * scale
    return y


def _proj_kernel(*refs, epilogues, rope):
    n_out = len(epilogues)
    x_ref, gain_ref, sc_ref, sh_ref = refs[:4]
    pos = 4
    if rope:
        cos_ref, sin_ref = refs[pos:pos + 2]
        pos += 2
    qkg_ref = refs[pos]
    pos += 1
    w_refs = refs[pos:pos + n_out]
    o_refs = refs[pos + n_out:pos + 2 * n_out]

    h = _norm_mod(x_ref[0], gain_ref[...], sc_ref[0], sh_ref[0]).astype(bf16)
    for w_ref, o_ref, epi in zip(w_refs, o_refs, epilogues):
        acc = _dot(h, w_ref[...])
        kind = epi[0]
        if kind == "plain":
            o_ref[0] = acc.astype(o_ref.dtype)
        elif kind == "submap":
            _, grow, use_rope, scale = epi
            gain = qkg_ref[grow:grow + 1, :]
            for hd in range(acc.shape[1] // LANES):
                t = acc[:, hd * LANES:(hd + 1) * LANES]
                cs = (cos_ref[...], sin_ref[...]) if (rope and use_rope) else (None, None)
                y = _submap_norm_rope(t, gain, cs[0], cs[1], scale)
                o_ref[0, :, hd * LANES:(hd + 1) * LANES] = y.astype(o_ref.dtype)
        elif kind == "headnorm":
            _, grow, scale = epi
            gain = qkg_ref[grow:grow + 1, :]
            for hd in range(acc.shape[1] // LANES):
                t = acc[:, hd * LANES:(hd + 1) * LANES]
                o_ref[0, :, hd * LANES:(hd + 1) * LANES] = _head_norm(t, gain, scale).astype(o_ref.dtype)
        else:
            raise ValueError(kind)


def norm_mod_project(x, gain, sc, sh, weights, out_dtypes, epilogues, qk_gain, rope_tabs=None,
                     tm=512, name="proj"):
    b, l, d = x.shape
    tm = min(tm, l)
    assert l % tm == 0
    bm = sc.shape[0]
    mod_map = (lambda bi, i: (bi, 0, 0)) if bm == b else (lambda bi, i: (0, 0, 0))
    rope = rope_tabs is not None
    in_specs = [
        pl.BlockSpec((1, tm, d), lambda bi, i: (bi, i, 0)),
        pl.BlockSpec((1, d), lambda bi, i: (0, 0)),
        pl.BlockSpec((1, 1, d), mod_map),
        pl.BlockSpec((1, 1, d), mod_map),
    ]
    args = [x, gain, sc, sh]
    if rope:
        in_specs += [pl.BlockSpec((tm, LANES), lambda bi, i: (i, 0))] * 2
        args += list(rope_tabs)
    in_specs.append(pl.BlockSpec(qk_gain.shape, lambda bi, i: (0, 0)))
    args.append(qk_gain)
    for w in weights:
        in_specs.append(pl.BlockSpec(w.shape, lambda bi, i: (0, 0)))
        args.append(w)
    out_shape = [jax.ShapeDtypeStruct((b, l, w.shape[1]), dt) for w, dt in zip(weights, out_dtypes)]
    out_specs = [pl.BlockSpec((1, tm, w.shape[1]), lambda bi, i: (bi, i, 0)) for w in weights]
    return pl.pallas_call(
        functools.partial(_proj_kernel, epilogues=tuple(epilogues), rope=rope),
        out_shape=out_shape,
        grid=(b, l // tm),
        in_specs=in_specs,
        out_specs=out_specs,
        compiler_params=_cparams(("parallel", "parallel")),
        name=name,
    )(*args)


def _diff_attn_kernel(*refs, has_lat, tk, lam_init):
    if has_lat:
        lam_ref, q_ref, kc_ref, vc_ref, k_ref, v_ref, subln_ref, o_ref = refs
    else:
        lam_ref, q_ref, kc_ref, vc_ref, subln_ref, o_ref = refs
    q = q_ref[0]
    tq = q.shape[0]
    lane = lax.broadcasted_iota(jnp.int32, q.shape, 1)
    zero = jnp.zeros_like(q)
    q1 = jnp.where(lane < DH_A, q, zero)
    q2 = jnp.where(lane < DH_A, zero, q)

    def one_map(qm, kblk, vblk, m, l, acc):
        s = _dot_nt(qm, kblk)
        m_new = jnp.maximum(m, jnp.max(s, axis=-1, keepdims=True))
        p = jnp.exp(s - m_new)
        a = jnp.exp(m - m_new)
        l = a * l + jnp.sum(p, axis=-1, keepdims=True)
        acc = a * acc + _dot(p.astype(bf16), vblk)
        return m_new, l, acc

    def step(kblk, vblk, carry):
        m1, l1, a1, m2, l2, a2 = carry
        m1, l1, a1 = one_map(q1, kblk, vblk, m1, l1, a1)
        m2, l2, a2 = one_map(q2, kblk, vblk, m2, l2, a2)
        return m1, l1, a1, m2, l2, a2

    mi = jnp.full((tq, 1), NEG, f32)
    li = jnp.zeros((tq, 1), f32)
    ai = jnp.zeros((tq, DV_A), f32)
    carry = step(kc_ref[0], vc_ref[0], (mi, li, ai, mi, li, ai))
    if has_lat:
        n_blk = k_ref.shape[1] // tk

        def body(j, c):
            off = pl.multiple_of(j * tk, tk)
            return step(k_ref[0, pl.ds(off, tk), :], v_ref[0, pl.ds(off, tk), :], c)

        carry = lax.fori_loop(0, n_blk, body, carry)
    m1, l1, a1, m2, l2, a2 = carry
    lam = lam_ref[0]
    o = a1 * (1.0 / l1) - lam * (a2 * (1.0 / l2))
    y = o * lax.rsqrt(jnp.mean(o * o, axis=-1, keepdims=True) + EPS) * subln_ref[...]
    o_ref[0] = (y * (1.0 - lam_init)).astype(o_ref.dtype)


def diff_attention(lam, q, k_ctx, v_ctx, k_lat, v_lat, subln, lam_init, tq=512, tk=512, name="diff_attn"):
    b, lq, _ = q.shape
    lc = k_ctx.shape[1]
    has_lat = k_lat is not None
    tq = min(tq, lq)
    assert lq % tq == 0
    head_blk = lambda rows: pl.BlockSpec((1, rows, LANES), lambda bi, h, i: (bi, 0, h))
    in_specs = [
        pl.BlockSpec(memory_space=pltpu.SMEM),
        pl.BlockSpec((1, tq, LANES), lambda bi, h, i: (bi, i, h)),
        head_blk(lc), head_blk(lc),
    ]
    args = [lam, q, k_ctx, v_ctx]
    if has_lat:
        ll = k_lat.shape[1]
        tk = min(tk, ll)
        assert ll % tk == 0
        in_specs += [head_blk(ll), head_blk(ll)]
        args += [k_lat, v_lat]
    in_specs.append(pl.BlockSpec((1, LANES), lambda bi, h, i: (0, 0)))
    args.append(subln)
    return pl.pallas_call(
        functools.partial(_diff_attn_kernel, has_lat=has_lat, tk=tk, lam_init=lam_init),
        out_shape=jax.ShapeDtypeStruct((b, lq, A_W), bf16),
        grid=(b, H_A, lq // tq),
        in_specs=in_specs,
        out_specs=pl.BlockSpec((1, tq, LANES), lambda bi, h, i: (bi, i, h)),
        compiler_params=_cparams(("parallel", "parallel", "arbitrary")),
        name=name,
    )(*args)


def _gdn_conv_kernel(x_ref, w_ref, o_ref, pad_ref, *, rows_per_step):
    l = x_ref.shape[1]
    half = CONV_K // 2
    halo = 8
    zeros = jnp.zeros((halo, LANES), f32)
    pad_ref[0:halo, :] = zeros
    pad_ref[halo + l:halo + l + halo, :] = zeros
    pad_ref[halo:halo + l, :] = x_ref[0]
    kind = pl.program_id(1) // H_B
    w = w_ref[...]
    r = rows_per_step

    def body(i, _):
        t0 = pl.multiple_of(i * r, r)
        win = pad_ref[pl.ds(t0, r + 2 * halo), :]
        y = jnp.zeros((r, LANES), f32)
        for j in range(CONV_K):
            s = halo - half + j
            y = y + win[s:s + r, :] * w[j:j + 1, :]
        y = _silu(y)
        nrm = lax.rsqrt(jnp.sum(y * y, axis=-1, keepdims=True) + EPS)
        nrm = jnp.where(kind == 0, nrm * (DH_B ** -0.5), nrm)
        y = jnp.where(kind == 2, y, y * nrm)
        o_ref[0, pl.ds(t0, r), :] = y
        return 0

    lax.fori_loop(0, l // r, body, 0)


def gdn_short_conv(qkv, conv_w):
    b, l, c = qkv.shape
    r = min(512, l)
    assert l % r == 0
    wpad = jnp.zeros((8, c), f32).at[:CONV_K].set(conv_w)
    return pl.pallas_call(
        functools.partial(_gdn_conv_kernel, rows_per_step=r),
        out_shape=jax.ShapeDtypeStruct((b, l, c), f32),
        grid=(b, c // LANES),
        in_specs=[
            pl.BlockSpec((1, l, LANES), lambda bi, j: (bi, 0, j)),
            pl.BlockSpec((8, LANES), lambda bi, j: (0, j)),
        ],
        out_specs=pl.BlockSpec((1, l, LANES), lambda bi, j: (bi, 0, j)),
        scratch_shapes=[pltpu.VMEM((l + 16, LANES), f32)],
        compiler_params=_cparams(("parallel", "parallel")),
        name="gdn_conv",
    )(qkv, wpad)


def _softplus(x):
    return jnp.maximum(x, 0.0) + jnp.log1p(jnp.exp(-jnp.abs(x)))


def _sigmoid(x):
    return 1.0 / (1.0 + jnp.exp(-x))


def _stack_heads(x, base):
    return jnp.concatenate([x[:, base + h * DH_B:base + (h + 1) * DH_B] for h in range(H_B)], axis=0)


def _gdn_chunk_prep(x, graw, prm, reverse):
    c = GDN_CHUNK
    n = H_B * c
    q = _stack_heads(x, 0)
    k = _stack_heads(x, B_W)
    v = _stack_heads(x, 2 * B_W)
    r = lax.broadcasted_iota(jnp.int32, (n, n), 0)
    cc = lax.broadcasted_iota(jnp.int32, (n, n), 1)
    same = (r // c) == (cc // c)
    eye = r == cc
    if reverse:
        incl, incl_t, strict = same & (r <= cc), same & (r >= cc), same & (r < cc)
    else:
        incl, incl_t, strict = same & (r >= cc), same & (r <= cc), same & (r > cc)

    beta_r = _sigmoid(graw[0:1, :])
    la_r = -jnp.exp(prm[0:1, :]) * _softplus(graw[1:2, :] + prm[1:2, :])
    to_col = lambda row: jnp.sum(jnp.where(eye, row, 0.0), axis=1, keepdims=True)
    beta_c = to_col(beta_r)
    la_c = to_col(la_r)
    g_c = jnp.sum(jnp.where(incl, la_r, 0.0), axis=1, keepdims=True)
    g_r = jnp.sum(jnp.where(incl_t, la_c, 0.0), axis=0, keepdims=True)
    gtot_c = jnp.sum(jnp.where(same, la_r, 0.0), axis=1, keepdims=True)
    decay = jnp.where(incl, jnp.exp(jnp.where(incl, g_c - g_r, 0.0)), 0.0)

    kb = k.astype(bf16)
    kq_k = _dot_nt(jnp.concatenate([kb, q.astype(bf16)], axis=0), kb)
    lmat = jnp.where(strict, beta_c * decay * kq_k[:n], 0.0)
    eg = jnp.exp(g_c)
    rhs = jnp.concatenate([beta_c * v, (beta_c * eg) * k], axis=-1)
    half = jnp.where((r // 2) == (cc // 2), lmat, 0.0)
    t = jnp.where(eye, 1.0, 0.0) - half
    kb_ = 2
    while kb_ < c:
        ck = jnp.where(((r // (2 * kb_)) == (cc // (2 * kb_))) & ((r // kb_) != (cc // kb_)), lmat, 0.0)
        t = t - _mm(_mm(t, ck), t)
        kb_ *= 2
    sol = _mm(t, rhs)
    u0, w = sol[:, :DH_B], sol[:, DH_B:]

    aqk = (kq_k[n:] * decay).astype(bf16)
    qg = q * eg
    kg = k * jnp.exp(gtot_c - g_c)
    hr = lax.broadcasted_iota(jnp.int32, (n, H_B * DH_B), 0) // c
    hc = lax.broadcasted_iota(jnp.int32, (n, H_B * DH_B), 1) // DH_B
    blk = hr == hc
    spread = lambda t: jnp.where(blk, jnp.concatenate([t] * H_B, axis=1), 0.0).astype(bf16)
    blk_t = (lax.broadcasted_iota(jnp.int32, (H_B * DH_B, n), 0) // DH_B
             == lax.broadcasted_iota(jnp.int32, (H_B * DH_B, n), 1) // c)
    kgt = jnp.concatenate([kg.T] * H_B, axis=0)
    kgt = jnp.where(blk_t, kgt, 0.0).astype(bf16)
    gl = jnp.exp(jnp.sum(jnp.where(blk_t, la_r, 0.0), axis=1, keepdims=True))
    return u0, jnp.concatenate([spread(w), spread(qg)], axis=0), jnp.concatenate([aqk, kgt], axis=0), gl


def _gdn_chunk_step(prep, s_st):
    u0, wq_bd, ak, gl = prep
    c = GDN_CHUNK
    n = H_B * c
    ws_qs = _dot(wq_bd, s_st.astype(bf16))
    ub = (u0 - ws_qs[:n]).astype(bf16)
    au_ku = _dot(ak, ub)
    o = ws_qs[n:] + au_ku[:n]
    s_new = gl * s_st + au_ku[n:]
    return jnp.concatenate([o[h * c:(h + 1) * c] for h in range(H_B)], axis=1), s_new


def _gdn_kernel(xf_ref, xb_ref, grf_ref, grb_ref, prm_ref, s0_ref, of_ref, ob_ref, s_ref, *, chunks):
    @pl.when(pl.program_id(1) == 0)
    def _():
        s_ref[...] = s0_ref[...]

    c = GDN_CHUNK
    rows = lambda j: slice(j * c, (j + 1) * c)
    prep_f = [_gdn_chunk_prep(xf_ref[0, rows(j)], grf_ref[0, j, 0:2], prm_ref[0:2], False) for j in range(chunks)]
    prep_b = [_gdn_chunk_prep(xb_ref[0, rows(j)], grb_ref[0, j, 2:4], prm_ref[2:4], True) for j in range(chunks)]
    s_f = s_ref[0, 0]
    s_b = s_ref[0, 1]
    for j in range(chunks):
        o, s_f = _gdn_chunk_step(prep_f[j], s_f)
        of_ref[0, rows(j)] = o
    for j in reversed(range(chunks)):
        o, s_b = _gdn_chunk_step(prep_b[j], s_b)
        ob_ref[0, rows(j)] = o
    s_ref[0, 0] = s_f
    s_ref[0, 1] = s_b


def gdn_scan(x, gates_row, prm, s0, chunks=2):
    b, l, _ = x.shape
    c = GDN_CHUNK
    assert l % (c * chunks) == 0
    nb = l // (c * chunks)
    rows = c * chunks
    fwd = lambda bi, i: (bi, i, 0)
    bwd = lambda bi, i: (bi, nb - 1 - i, 0)
    st_spec = pl.BlockSpec((1, 2, H_B * DH_B, DH_B), lambda bi, i: (bi, 0, 0, 0))
    g_blk = (1, chunks, 4, H_B * c)
    return pl.pallas_call(
        functools.partial(_gdn_kernel, chunks=chunks),
        out_shape=[jax.ShapeDtypeStruct((b, l, B_W), f32), jax.ShapeDtypeStruct((b, l, B_W), f32),
                   jax.ShapeDtypeStruct(s0.shape, f32)],
        grid=(b, nb),
        in_specs=[
            pl.BlockSpec((1, rows, 3 * B_W), fwd), pl.BlockSpec((1, rows, 3 * B_W), bwd),
            pl.BlockSpec(g_blk, lambda bi, i: (bi, i, 0, 0)),
            pl.BlockSpec(g_blk, lambda bi, i: (bi, nb - 1 - i, 0, 0)),
            pl.BlockSpec((4, H_B * c), lambda bi, i: (0, 0)),
            st_spec,
        ],
        out_specs=[pl.BlockSpec((1, rows, B_W), fwd), pl.BlockSpec((1, rows, B_W), bwd), st_spec],
        compiler_params=_cparams(("parallel", "arbitrary")),
        name="gdn_scan",
    )(x, x, gates_row, gates_row, prm, s0)


def _na_kernel(q_ref, k_ref, v_ref, kc_ref, vc_ref, bias_ref, o_ref, *, rb, band, rows):
    i = pl.program_id(2)
    ub = jnp.clip(i * rb - WIN_R // 2, 0, rows - band)
    off = pl.multiple_of(ub * GRID_W, GRID_W)
    q = q_ref[0]
    kb = k_ref[0, pl.ds(off, band * GRID_W), :]
    vb = v_ref[0, pl.ds(off, band * GRID_W), :]
    s_lat = _dot_nt(q, kb) + bias_ref[0, 0]
    s_ctx = _dot_nt(q, kc_ref[0])
    m = jnp.maximum(jnp.max(s_lat, axis=-1, keepdims=True), jnp.max(s_ctx, axis=-1, keepdims=True))
    p_lat = jnp.exp(s_lat - m)
    p_ctx = jnp.exp(s_ctx - m)
    den = jnp.sum(p_lat, axis=-1, keepdims=True) + jnp.sum(p_ctx, axis=-1, keepdims=True)
    o = _dot(p_lat.astype(bf16), vb) + _dot(p_ctx.astype(bf16), vc_ref[0])
    o_ref[0] = (o * (1.0 / den)).astype(o_ref.dtype)


def _na_bias_index(rows, rb, band):
    wr = min(WIN_R, rows)
    nblk = rows // rb
    cols = np.arange(GRID_W)
    c_start = np.clip(cols - WIN_C // 2, 0, GRID_W - WIN_C)
    col_ok = (cols[None, :] >= c_start[:, None]) & (cols[None, :] < c_start[:, None] + WIN_C)
    dc_idx = np.clip(cols[None, :] - cols[:, None] + WIN_C - 1, 0, 2 * WIN_C - 2)

    def geometry(i):
        ub = int(np.clip(i * rb - WIN_R // 2, 0, rows - band))
        qr = i * rb + np.arange(rb)
        kr = ub + np.arange(band)
        r_start = np.clip(qr - wr // 2, 0, rows - wr)
        row_ok = (kr[None, :] >= r_start[:, None]) & (kr[None, :] < r_start[:, None] + wr)
        dr = np.clip(kr[None, :] - qr[:, None] + WIN_R - 1, 0, 2 * WIN_R - 2)
        ok = row_ok[:, None, :, None] & col_ok[None, :, None, :]
        return ok.reshape(rb * GRID_W, band * GRID_W), dr

    reps = [0, min(1, nblk - 1), nblk - 1]
    geo = [geometry(i) for i in reps]
    for i in range(1, nblk - 1):
        g = geometry(i)
        assert all(np.array_equal(a, b_) for a, b_ in zip(g, geo[1]))
    ok = np.stack([g[0] for g in geo])
    dr = np.stack([g[1] for g in geo])
    dr_onehot = (dr[..., None] == np.arange(2 * WIN_R - 1)).astype(np.float32)
    dc_onehot = (dc_idx[None] == np.arange(2 * WIN_C - 1)[:, None, None]).astype(np.float32)
    return ok, dr_onehot, dc_onehot


def _na_bias_table(rpb, rows, rb, band):
    ok, dr_onehot, dc_onehot = _na_bias_index(rows, rb, band)
    hi = lax.Precision.HIGHEST
    by_col = jnp.einsum('hrc,cqk->hrqk', rpb.astype(f32), dc_onehot, precision=hi)
    slabs = jnp.einsum('vabr,hrqk->vhaqbk', dr_onehot, by_col, precision=hi)
    h = rpb.shape[0]
    return jnp.where(ok[:, None], slabs.reshape(3, h, rb * GRID_W, band * GRID_W), NEG)


def neighbourhood_attention(q, k, v, k_ctx, v_ctx, rpb, rb=4):
    b, l, _ = q.shape
    lc = k_ctx.shape[1]
    rows = l // GRID_W
    band = rb + WIN_R - 1
    assert rows % rb == 0 and rows >= band and WIN_R <= rows
    nblk = rows // rb
    bias = _na_bias_table(rpb, rows, rb, band)
    tq = rb * GRID_W
    variant = lambda i: jnp.where(i == 0, 0, jnp.where(i == nblk - 1, 2, 1))
    head_blk = lambda n: pl.BlockSpec((1, n, LANES), lambda bi, h, i: (bi, 0, h))
    return pl.pallas_call(
        functools.partial(_na_kernel, rb=rb, band=band, rows=rows),
        out_shape=jax.ShapeDtypeStruct((b, l, H_C * DH_C), bf16),
        grid=(b, H_C, nblk),
        in_specs=[
            pl.BlockSpec((1, tq, LANES), lambda bi, h, i: (bi, i, h)),
            head_blk(l), head_blk(l), head_blk(lc), head_blk(lc),
            pl.BlockSpec((1, 1, tq, band * GRID_W), lambda bi, h, i: (variant(i), h, 0, 0)),
        ],
        out_specs=pl.BlockSpec((1, tq, LANES), lambda bi, h, i: (bi, i, h)),
        compiler_params=_cparams(("parallel", "parallel", "arbitrary")),
        name="na_attn",
    )(q, k, v, k_ctx, v_ctx, bias)


def _outproj_kernel(*refs, has_gdn):
    if has_gdn:
        x_ref, g_ref, a_ref, wa_ref, of_ref, ob_ref, gate_ref, gain_ref, wb_ref, o_ref = refs
    else:
        x_ref, g_ref, a_ref, wa_ref, o_ref = refs
    acc = _dot(a_ref[0], wa_ref[...])
    if has_gdn:
        o = of_ref[0] + ob_ref[0]
        gate = gate_ref[0]
        parts = []
        for h in range(H_B):
            sl = slice(h * DH_B, (h + 1) * DH_B)
            t = o[:, sl]
            y = t * lax.rsqrt(jnp.mean(t * t, axis=-1, keepdims=True) + EPS) * gain_ref[...]
            parts.append((y * _silu(gate[:, sl])).astype(bf16))
        acc = acc + _dot(jnp.concatenate(parts, axis=-1), wb_ref[...])
    o_ref[0] = x_ref[0] + g_ref[0] * acc


def out_project(x, g, a, wa, gdn=None, tm=512, name="outproj"):
    b, l, d = x.shape
    tm = min(tm, l)
    assert l % tm == 0
    bm = g.shape[0]
    mod_map = (lambda bi, i: (bi, 0, 0)) if bm == b else (lambda bi, i: (0, 0, 0))
    row = lambda n: pl.BlockSpec((1, tm, n), lambda bi, i: (bi, i, 0))
    full = lambda arr: pl.BlockSpec(arr.shape, lambda bi, i: (0, 0))
    in_specs = [row(d), pl.BlockSpec((1, 1, d), mod_map), row(a.shape[2]), full(wa)]
    args = [x, g, a, wa]
    if gdn is not None:
        o_f, o_b, gate, gain, wb = gdn
        in_specs += [row(B_W), row(B_W), row(B_W), full(gain), full(wb)]
        args += [o_f, o_b, gate, gain, wb]
    return pl.pallas_call(
        functools.partial(_outproj_kernel, has_gdn=gdn is not None),
        out_shape=jax.ShapeDtypeStruct((b, l, d), f32),
        grid=(b, l // tm),
        in_specs=in_specs,
        out_specs=row(d),
        compiler_params=_cparams(("parallel", "parallel")),
        name=name,
    )(*args)


def _ffn_kernel(x_ref, gain_ref, sc_ref, sh_ref, g_ref, wg_ref, wu_ref, wo_ref, o_ref, h_ref, acc_ref):
    j = pl.program_id(2)

    @pl.when(j == 0)
    def _():
        h_ref[...] = _norm_mod(x_ref[0], gain_ref[...], sc_ref[0], sh_ref[0]).astype(bf16)
        acc_ref[...] = jnp.zeros_like(acc_ref)

    h = h_ref[...]
    gt = _dot(h, wg_ref[...])
    up = _dot(h, wu_ref[...])
    acc_ref[...] += _dot((_silu(gt) * up).astype(bf16), wo_ref[...])

    @pl.when(j == pl.num_programs(2) - 1)
    def _():
        o_ref[0] = x_ref[0] + g_ref[0] * acc_ref[...]


def ffn(x, gain, sc, sh, g, w_in, w_out, tm=512, tf=256, name="ffn"):
    b, l, d = x.shape
    ff = w_out.shape[0]
    tm = min(tm, l)
    assert l % tm == 0 and ff % tf == 0
    nf = ff // tf
    bm = sc.shape[0]
    mod_map = (lambda bi, i, j: (bi, 0, 0)) if bm == b else (lambda bi, i, j: (0, 0, 0))
    mod = pl.BlockSpec((1, 1, d), mod_map)
    return pl.pallas_call(
        _ffn_kernel,
        out_shape=jax.ShapeDtypeStruct((b, l, d), f32),
        grid=(b, l // tm, nf),
        in_specs=[
            pl.BlockSpec((1, tm, d), lambda bi, i, j: (bi, i, 0)),
            pl.BlockSpec((1, d), lambda bi, i, j: (0, 0)),
            mod, mod, mod,
            pl.BlockSpec((d, tf), lambda bi, i, j: (0, j)),
            pl.BlockSpec((d, tf), lambda bi, i, j: (0, j + nf)),
            pl.BlockSpec((tf, d), lambda bi, i, j: (j, 0)),
        ],
        out_specs=pl.BlockSpec((1, tm, d), lambda bi, i, j: (bi, i, 0)),
        scratch_shapes=[pltpu.VMEM((tm, d), bf16), pltpu.VMEM((tm, d), f32)],
        compiler_params=_cparams(("parallel", "parallel", "arbitrary")),
        name=name,
    )(x, gain, sc, sh, g, w_in, w_in, w_out)


def _rope_tables(l):
    n_freq = DH_A // 4
    inv = ROPE_BASE ** (-jnp.arange(n_freq, dtype=f32) / n_freq)
    t = jnp.arange(l)
    row = (t // GRID_W).astype(f32)
    col = (t % GRID_W).astype(f32)
    ang = jnp.concatenate([row[:, None] * inv, col[:, None] * inv], axis=-1)
    cos = jnp.repeat(jnp.cos(ang), 2, axis=-1)
    sin = jnp.repeat(jnp.sin(ang), 2, axis=-1)
    sign = jnp.tile(jnp.array([-1.0, 1.0], f32), DH_A // 2)
    return jnp.tile(cos, (1, 2)), jnp.tile(sin * sign, (1, 2))


def _pad_rows(a, rows):
    return jnp.zeros((rows,) + a.shape[1:], a.dtype).at[:a.shape[0]].set(a)


def kernel(x, c, ctx, c_ctx, ada_w, ada_b, norm_mix, norm_ffn, ffn_w_in, ffn_w_out, even_w_in, even_w_out,
           diff_qk_gain, diff_lambda, diff_subln, gdn_conv, gdn_a_log, gdn_dt_bias, gdn_norm, odd_w_in,
           odd_w_out, na_qk_gain, na_rpb):
    b, l, d = x.shape
    lc = ctx.shape[1]
    depth = ada_w.shape[0]
    c_all = _pad_rows(jnp.concatenate([c, c_ctx[None]], axis=0), 16)
    mods = ada_modulation(c_all, ada_w, ada_b)
    cos_t, sin_t = _rope_tables(l)
    ctx_flat = ctx.reshape(1, b * lc, d)

    for layer in range(depth):
        ctx_out = layer < depth - 1
        m_lat = [t[:, None, :] for t in jnp.split(mods[layer, :b], 6, axis=-1)]
        m_ctx = [t[:, None, :] for t in jnp.split(mods[layer, b:b + 1], 6, axis=-1)]
        sh_m, sc_m, g_m, sh_f, sc_f, g_f = m_lat
        csh_m, csc_m, cg_m, csh_f, csc_f, cg_f = m_ctx
        gain_m = norm_mix[layer][None]
        gain_f = norm_ffn[layer][None]
        if layer % 2 == 0:
            e = layer // 2
            lam_init = 0.8 - 0.6 * math.exp(-0.3 * layer)
            w_in = even_w_in[e].astype(bf16)
            cuts = [0, A_W, 2 * A_W, 3 * A_W, 3 * A_W + 3 * B_W, 3 * A_W + 4 * B_W]
            ws = [w_in[:, cuts[i]:cuts[i + 1]] for i in range(5)]
            w_gates = jnp.zeros((d, LANES), bf16).at[:, :4 * H_B].set(w_in[:, cuts[5]:])
            ws.append(w_gates)
            qk_gain = _pad_rows(jnp.tile(diff_qk_gain[e], (1, 2)), 8)
            dts = [bf16, bf16, bf16, f32, f32, f32]
            epi = lambda rope: [("submap", 0, rope, DH_A ** -0.5), ("submap", 1, rope, 1.0), ("plain",),
                                ("plain",), ("plain",), ("plain",)]
            qa, ka, va, qkv_b, g_b, gates = norm_mod_project(
                x, gain_m, sc_m, sh_m, ws, dts, epi(True), qk_gain, rope_tabs=(cos_t, sin_t), name="even_proj")
            qac, kac, vac, qkv_bc, g_bc, gates_c = norm_mod_project(
                ctx_flat, gain_m, csc_m, csh_m, ws, dts, epi(False), qk_gain, name="even_proj_ctx")
            unflat = lambda t: t.reshape(b, lc, t.shape[-1])
            qac, kac, vac, qkv_bc, g_bc, gates_c = map(unflat, (qac, kac, vac, qkv_bc, g_bc, gates_c))

            lv = diff_lambda[e]
            lam = (jnp.exp(jnp.sum(lv[0] * lv[1])) - jnp.exp(jnp.sum(lv[2] * lv[3])) + lam_init).reshape(1)
            subln = diff_subln[e][None]
            a_lat = diff_attention(lam, qa, kac, vac, ka, va, subln, lam_init, name="diff_attn")
            a_ctx = diff_attention(lam, qac, kac, vac, None, None, subln, lam_init, name="diff_attn_ctx")

            y_lat = gdn_short_conv(qkv_b, gdn_conv[e])
            y_ctx = gdn_short_conv(qkv_bc, gdn_conv[e])
            per_pos = lambda p: jnp.repeat(p, GDN_CHUNK)
            prm = jnp.stack([per_pos(gdn_a_log[e][0]), per_pos(gdn_dt_bias[e][0]),
                             per_pos(gdn_a_log[e][1]), per_pos(gdn_dt_bias[e][1])])

            def to_rows(g):
                t = g[:, :, :4 * H_B].reshape(g.shape[0], -1, GDN_CHUNK, 4, H_B).transpose(0, 1, 3, 4, 2)
                t = t.reshape(g.shape[0], -1, 4, H_B * GDN_CHUNK)
                return t[:, :, jnp.array([0, 2, 1, 3])]

            s0 = jnp.zeros((b, 2, H_B * DH_B, DH_B), f32)
            ocf, ocb, s_mid = gdn_scan(y_ctx, to_rows(gates_c), prm, s0)
            olf, olb, _ = gdn_scan(y_lat, to_rows(gates), prm, s_mid)

            w_out = even_w_out[e].astype(bf16)
            gdn_gain = gdn_norm[e][None]
            x = out_project(x, g_m, a_lat, w_out[:A_W], gdn=(olf, olb, g_b, gdn_gain, w_out[A_W:]),
                            name="even_out")
            ctx_flat = out_project(
                ctx_flat, cg_m, a_ctx.reshape(1, b * lc, A_W), w_out[:A_W],
                gdn=(ocf.reshape(1, b * lc, B_W), ocb.reshape(1, b * lc, B_W), g_bc.reshape(1, b * lc, B_W),
                     gdn_gain, w_out[A_W:]), name="even_out_ctx")
        else:
            od = layer // 2
            w_in = odd_w_in[od].astype(bf16)
            mix = H_C * DH_C
            ws = [w_in[:, :mix], w_in[:, mix:2 * mix], w_in[:, 2 * mix:]]
            qk_gain = _pad_rows(na_qk_gain[od], 8)
            q, k, v = norm_mod_project(
                x, gain_m, sc_m, sh_m, ws, [bf16] * 3,
                [("headnorm", 0, DH_C ** -0.5), ("headnorm", 1, 1.0), ("plain",)], qk_gain, name="odd_proj")
            kc, vc = norm_mod_project(
                ctx_flat, gain_m, csc_m, csh_m, ws[1:], [bf16] * 2,
                [("headnorm", 1, 1.0), ("plain",)], qk_gain, name="odd_proj_ctx")
            kc, vc = kc.reshape(b, lc, mix), vc.reshape(b, lc, mix)
            o = neighbourhood_attention(q, k, v, kc, vc, na_rpb[od])
            x = out_project(x, g_m, o, odd_w_out[od].astype(bf16), name="odd_out")
            if ctx_out:
                raise NotImplementedError("context output of a neighbourhood layer is not needed at depth 2")

        wf_in = ffn_w_in[layer].astype(bf16)
        wf_out = ffn_w_out[layer].astype(bf16)
        x = ffn(x, gain_f, sc_f, sh_f, g_f, wf_in, wf_out, name="ffn")
        if ctx_out:
            ctx_flat = ffn(ctx_flat, gain_f, csc_f, csh_f, cg_f, wf_in, wf_out, name="ffn_ctx")
    return x
```

```python
import functools
import math

import numpy as np
import jax
import jax.numpy as jnp
from jax import lax
from jax.experimental import pallas as pl
from jax.experimental.pallas import tpu as pltpu

f32 = jnp.float32
bf16 = jnp.bfloat16

EPS = 1e-6
ROPE_BASE = 10000.0
GRID_W = 64
H_A = 4
DH_A = 64
DV_A = 2 * DH_A
A_W = H_A * DV_A
H_B = 4
DH_B = 128
B_W = H_B * DH_B
GDN_CHUNK = 64
CONV_K = 5
H_C = 8
DH_C = 128
WIN_R = 8
WIN_C = 16

LANES = 128
VMEM_LIMIT = 56 * 1024 * 1024
NEG = -1e30


def _cparams(sem):
    return pltpu.CompilerParams(dimension_semantics=sem, vmem_limit_bytes=VMEM_LIMIT)


def _silu(x):
    return x * (1.0 / (1.0 + jnp.exp(-x)))


def _dot(a, b):
    return jnp.dot(a, b, preferred_element_type=f32)


def _dot_nt(a, b):
    return lax.dot_general(a, b, (((1,), (1,)), ((), ())), preferred_element_type=f32)


def _split_bf16(x):
    hi = x.astype(bf16)
    lo = (x - hi.astype(f32)).astype(bf16)
    return hi, lo


def _dot3(a, b):
    ah, al = _split_bf16(a)
    bh, bl = _split_bf16(b)
    return _dot(ah, bh) + (_dot(ah, bl) + _dot(al, bh))


def _mm(a, b):
    return _dot(a.astype(bf16), b.astype(bf16))


def _dot3_nt(a, b):
    ah, al = _split_bf16(a)
    bh, bl = _split_bf16(b)
    return _dot_nt(ah, bh) + (_dot_nt(ah, bl) + _dot_nt(al, bh))


def _ada_kernel(c_ref, w_ref, b_ref, o_ref):
    sc = _silu(c_ref[...])
    o_ref[0] = _dot3(sc, w_ref[0]) + b_ref[0]


def ada_modulation(c_all, ada_w, ada_b):
    depth, d, n = ada_w.shape
    rows = c_all.shape[0]
    tn = 1536
    return pl.pallas_call(
        _ada_kernel,
        out_shape=jax.ShapeDtypeStruct((depth, rows, n), f32),
        grid=(depth, n // tn),
        in_specs=[
            pl.BlockSpec((rows, d), lambda l, j: (0, 0)),
            pl.BlockSpec((1, d, tn), lambda l, j: (l, 0, j)),
            pl.BlockSpec((1, 1, tn), lambda l, j: (l, 0, j)),
        ],
        out_specs=pl.BlockSpec((1, rows, tn), lambda l, j: (l, 0, j)),
        compiler_params=_cparams(("parallel", "parallel")),
        name="ada_modulation",
    )(c_all, ada_w, ada_b.reshape(depth, 1, n))


def _norm_mod(x, gain, sc, sh):
    y = x * lax.rsqrt(jnp.mean(x * x, axis=-1, keepdims=True) + EPS)
    return (y * gain) * (1.0 + sc) + sh


def _swap_pairs(x):
    lane = lax.broadcasted_iota(jnp.int32, x.shape, x.ndim - 1)
    nxt = pltpu.roll(x, x.shape[-1] - 1, x.ndim - 1)
    prv = pltpu.roll(x, 1, x.ndim - 1)
    return jnp.where(lane % 2 == 0, nxt, prv)


def _submap_norm_rope(t, gain, cos, sin, scale):
    lane = lax.broadcasted_iota(jnp.int32, t.shape, 1)
    lo = lane < DH_A
    sq = t * t
    s_lo = jnp.sum(jnp.where(lo, sq, 0.0), axis=-1, keepdims=True)
    s_hi = jnp.sum(jnp.where(lo, 0.0, sq), axis=-1, keepdims=True)
    r = jnp.where(lo, lax.rsqrt(s_lo * (1.0 / DH_A) + EPS), lax.rsqrt(s_hi * (1.0 / DH_A) + EPS))
    y = t * r * gain
    if cos is not None:
        y = y * cos + _swap_pairs(y) * sin
    if scale != 1.0:
        y = y * scale
    return y


def _head_norm(t, gain, scale):
    y = t * lax.rsqrt(jnp.mean(t * t, axis=-1, keepdims=True) + EPS) * gain
    if scale != 1.0:
        y = y * scale
    return y


def _proj_kernel(*refs, epilogues, rope):
    n_out = len(epilogues)
    x_ref, gain_ref, sc_ref, sh_ref = refs[:4]
    pos = 4
    if rope:
        cos_ref, sin_ref = refs[pos:pos + 2]
        pos += 2
    qkg_ref = refs[pos]
    pos += 1
    w_refs = refs[pos:pos + n_out]
    o_refs = refs[pos + n_out:pos + 2 * n_out]

    h = _norm_mod(x_ref[0], gain_ref[...], sc_ref[0], sh_ref[0]).astype(bf16)
    for w_ref, o_ref, epi in zip(w_refs, o_refs, epilogues):
        acc = _dot(h, w_ref[...])
        kind = epi[0]
        if kind == "plain":
            o_ref[0] = acc.astype(o_ref.dtype)
        elif kind == "submap":
            _, grow, use_rope, scale = epi
            gain = qkg_ref[grow:grow + 1, :]
            for hd in range(acc.shape[1] // LANES):
                t = acc[:, hd * LANES:(hd + 1) * LANES]
                cs = (cos_ref[...], sin_ref[...]) if (rope and use_rope) else (None, None)
                y = _submap_norm_rope(t, gain, cs[0], cs[1], scale)
                o_ref[0, :, hd * LANES:(hd + 1) * LANES] = y.astype(o_ref.dtype)
        elif kind == "headnorm":
            _, grow, scale = epi
            gain = qkg_ref[grow:grow + 1, :]
            for hd in range(acc.shape[1] // LANES):
                t = acc[:, hd * LANES:(hd + 1) * LANES]
                o_ref[0, :, hd * LANES:(hd + 1) * LANES] = _head_norm(t, gain, scale).astype(o_ref.dtype)
        else:
            raise ValueError(kind)


def norm_mod_project(x, gain, sc, sh, weights, out_dtypes, epilogues, qk_gain, rope_tabs=None,
                     tm=512, name="proj"):
    b, l, d = x.shape
    tm = min(tm, l)
    assert l % tm == 0
    bm = sc.shape[0]
    mod_map = (lambda bi, i: (bi, 0, 0)) if bm == b else (lambda bi, i: (0, 0, 0))
    rope = rope_tabs is not None
    in_specs = [
        pl.BlockSpec((1, tm, d), lambda bi, i: (bi, i, 0)),
        pl.BlockSpec((1, d), lambda bi, i: (0, 0)),
        pl.BlockSpec((1, 1, d), mod_map),
        pl.BlockSpec((1, 1, d), mod_map),
    ]
    args = [x, gain, sc, sh]
    if rope:
        in_specs += [pl.BlockSpec((tm, LANES), lambda bi, i: (i, 0))] * 2
        args += list(rope_tabs)
    in_specs.append(pl.BlockSpec(qk_gain.shape, lambda bi, i: (0, 0)))
    args.append(qk_gain)
    for w in weights:
        in_specs.append(pl.BlockSpec(w.shape, lambda bi, i: (0, 0)))
        args.append(w)
    out_shape = [jax.ShapeDtypeStruct((b, l, w.shape[1]), dt) for w, dt in zip(weights, out_dtypes)]
    out_specs = [pl.BlockSpec((1, tm, w.shape[1]), lambda bi, i: (bi, i, 0)) for w in weights]
    return pl.pallas_call(
        functools.partial(_proj_kernel, epilogues=tuple(epilogues), rope=rope),
        out_shape=out_shape,
        grid=(b, l // tm),
        in_specs=in_specs,
        out_specs=out_specs,
        compiler_params=_cparams(("parallel", "parallel")),
        name=name,
    )(*args)


def _diff_attn_kernel(*refs, has_lat, tk, lam_init, bounded):
    if has_lat:
        sc_ref, q_ref, kc_ref, vc_ref, k_ref, v_ref, subln_ref, o_ref = refs
    else:
        sc_ref, q_ref, kc_ref, vc_ref, subln_ref, o_ref = refs
    q = q_ref[0]
    tq = q.shape[0]
    lane = lax.broadcasted_iota(jnp.int32, q.shape, 1)
    zero = jnp.zeros_like(q)
    q1 = jnp.where(lane < DH_A, q, zero)
    q2 = jnp.where(lane < DH_A, zero, q)

    def kv_loop(step, carry):
        carry = step(kc_ref[0], vc_ref[0], carry)
        if has_lat:
            def body(j, c):
                off = pl.multiple_of(j * tk, tk)
                return step(k_ref[0, pl.ds(off, tk), :], v_ref[0, pl.ds(off, tk), :], c)

            carry = lax.fori_loop(0, k_ref.shape[1] // tk, body, carry)
        return carry

    if bounded:
        bound = sc_ref[1]

        def step(kblk, vblk, carry):
            ones_col = jnp.where(lax.broadcasted_iota(jnp.int32, vblk.shape, 1) == 0, 1.0, 0.0).astype(bf16)
            v_aug = jnp.concatenate([vblk, ones_col], axis=1)
            a1, a2 = carry
            a1 = a1 + _dot(jnp.exp2(_dot_nt(q1, kblk) - bound).astype(bf16), v_aug)
            a2 = a2 + _dot(jnp.exp2(_dot_nt(q2, kblk) - bound).astype(bf16), v_aug)
            return a1, a2

        ai = jnp.zeros((tq, 2 * DV_A), f32)
        a1, a2 = kv_loop(step, (ai, ai))
        l1 = jnp.sum(a1[:, DV_A:], axis=-1, keepdims=True)
        l2 = jnp.sum(a2[:, DV_A:], axis=-1, keepdims=True)
        a1, a2 = a1[:, :DV_A], a2[:, :DV_A]
    else:
        def one_map(qm, kblk, vblk, m, l, acc):
            s = _dot_nt(qm, kblk)
            m_new = jnp.maximum(m, jnp.max(s, axis=-1, keepdims=True))
            p = jnp.exp2(s - m_new)
            a = jnp.exp2(m - m_new)
            l = a * l + jnp.sum(p, axis=-1, keepdims=True)
            acc = a * acc + _dot(p.astype(bf16), vblk)
            return m_new, l, acc

        def step(kblk, vblk, carry):
            m1, l1, a1, m2, l2, a2 = carry
            m1, l1, a1 = one_map(q1, kblk, vblk, m1, l1, a1)
            m2, l2, a2 = one_map(q2, kblk, vblk, m2, l2, a2)
            return m1, l1, a1, m2, l2, a2

        mi = jnp.full((tq, 1), NEG, f32)
        li = jnp.zeros((tq, 1), f32)
        ai = jnp.zeros((tq, DV_A), f32)
        _, l1, a1, _, l2, a2 = kv_loop(step, (mi, li, ai, mi, li, ai))
    lam = sc_ref[0]
    o = a1 * (1.0 / l1) - lam * (a2 * (1.0 / l2))
    y = o * lax.rsqrt(jnp.mean(o * o, axis=-1, keepdims=True) + EPS) * subln_ref[...]
    o_ref[0] = (y * (1.0 - lam_init)).astype(o_ref.dtype)


def diff_attention(lam, q, k_ctx, v_ctx, k_lat, v_lat, subln, lam_init, bounded=False, tq=512, tk=512,
                   name="diff_attn"):
    b, lq, _ = q.shape
    lc = k_ctx.shape[1]
    has_lat = k_lat is not None
    tq = min(tq, lq)
    assert lq % tq == 0
    head_blk = lambda rows: pl.BlockSpec((1, rows, LANES), lambda bi, h, i: (bi, 0, h))
    in_specs = [
        pl.BlockSpec(memory_space=pltpu.SMEM),
        pl.BlockSpec((1, tq, LANES), lambda bi, h, i: (bi, i, h)),
        head_blk(lc), head_blk(lc),
    ]
    args = [lam, q, k_ctx, v_ctx]
    if has_lat:
        ll = k_lat.shape[1]
        tk = min(tk, ll)
        assert ll % tk == 0
        in_specs += [head_blk(ll), head_blk(ll)]
        args += [k_lat, v_lat]
    in_specs.append(pl.BlockSpec((1, LANES), lambda bi, h, i: (0, 0)))
    args.append(subln)
    return pl.pallas_call(
        functools.partial(_diff_attn_kernel, has_lat=has_lat, tk=tk, lam_init=lam_init, bounded=bounded),
        out_shape=jax.ShapeDtypeStruct((b, lq, A_W), bf16),
        grid=(b, H_A, lq // tq),
        in_specs=in_specs,
        out_specs=pl.BlockSpec((1, tq, LANES), lambda bi, h, i: (bi, i, h)),
        compiler_params=_cparams(("parallel", "parallel", "arbitrary")),
        name=name,
    )(*args)


def _gdn_conv_kernel(x_ref, w_ref, o_ref, pad_ref, *, rows_per_step):
    l = x_ref.shape[1]
    half = CONV_K // 2
    halo = 8
    zeros = jnp.zeros((halo, LANES), f32)
    pad_ref[0:halo, :] = zeros
    pad_ref[halo + l:halo + l + halo, :] = zeros
    pad_ref[halo:halo + l, :] = x_ref[0]
    kind = pl.program_id(1) // H_B
    w = w_ref[...]
    r = rows_per_step

    def body(i, _):
        t0 = pl.multiple_of(i * r, r)
        win = pad_ref[pl.ds(t0, r + 2 * halo), :]
        y = jnp.zeros((r, LANES), f32)
        for j in range(CONV_K):
            s = halo - half + j
            y = y + win[s:s + r, :] * w[j:j + 1, :]
        y = _silu(y)
        nrm = lax.rsqrt(jnp.sum(y * y, axis=-1, keepdims=True) + EPS)
        nrm = jnp.where(kind == 0, nrm * (DH_B ** -0.5), nrm)
        y = jnp.where(kind == 2, y, y * nrm)
        o_ref[0, pl.ds(t0, r), :] = y
        return 0

    lax.fori_loop(0, l // r, body, 0)


def gdn_short_conv(qkv, conv_w):
    b, l, c = qkv.shape
    r = min(512, l)
    assert l % r == 0
    wpad = jnp.zeros((8, c), f32).at[:CONV_K].set(conv_w)
    return pl.pallas_call(
        functools.partial(_gdn_conv_kernel, rows_per_step=r),
        out_shape=jax.ShapeDtypeStruct((b, l, c), f32),
        grid=(b, c // LANES),
        in_specs=[
            pl.BlockSpec((1, l, LANES), lambda bi, j: (bi, 0, j)),
            pl.BlockSpec((8, LANES), lambda bi, j: (0, j)),
        ],
        out_specs=pl.BlockSpec((1, l, LANES), lambda bi, j: (bi, 0, j)),
        scratch_shapes=[pltpu.VMEM((l + 16, LANES), f32)],
        compiler_params=_cparams(("parallel", "parallel")),
        name="gdn_conv",
    )(qkv, wpad)


def _softplus(x):
    return jnp.maximum(x, 0.0) + jnp.log1p(jnp.exp(-jnp.abs(x)))


def _sigmoid(x):
    return 1.0 / (1.0 + jnp.exp(-x))


def _stack_heads(x, base):
    return jnp.concatenate([x[:, base + h * DH_B:base + (h + 1) * DH_B] for h in range(H_B)], axis=0)


def _gdn_chunk_prep(x, graw, prm, reverse):
    c = GDN_CHUNK
    n = H_B * c
    q = _stack_heads(x, 0)
    k = _stack_heads(x, B_W)
    v = _stack_heads(x, 2 * B_W)
    r = lax.broadcasted_iota(jnp.int32, (n, n), 0)
    cc = lax.broadcasted_iota(jnp.int32, (n, n), 1)
    same = (r // c) == (cc // c)
    eye = r == cc
    if reverse:
        incl, incl_t, strict = same & (r <= cc), same & (r >= cc), same & (r < cc)
    else:
        incl, incl_t, strict = same & (r >= cc), same & (r <= cc), same & (r > cc)

    beta_r = _sigmoid(graw[0:1, :])
    la_r = -jnp.exp(prm[0:1, :]) * _softplus(graw[1:2, :] + prm[1:2, :])
    to_col = lambda row: jnp.sum(jnp.where(eye, row, 0.0), axis=1, keepdims=True)
    beta_c = to_col(beta_r)
    la_c = to_col(la_r)
    g_c = jnp.sum(jnp.where(incl, la_r, 0.0), axis=1, keepdims=True)
    g_r = jnp.sum(jnp.where(incl_t, la_c, 0.0), axis=0, keepdims=True)
    gtot_c = jnp.sum(jnp.where(same, la_r, 0.0), axis=1, keepdims=True)
    decay = jnp.where(incl, jnp.exp(jnp.where(incl, g_c - g_r, 0.0)), 0.0)

    kb = k.astype(bf16)
    kq_k = _dot_nt(jnp.concatenate([kb, q.astype(bf16)], axis=0), kb)
    lmat = jnp.where(strict, beta_c * decay * kq_k[:n], 0.0)
    eg = jnp.exp(g_c)
    rhs = jnp.concatenate([beta_c * v, (beta_c * eg) * k], axis=-1)
    half = jnp.where((r // 2) == (cc // 2), lmat, 0.0)
    t = jnp.where(eye, 1.0, 0.0) - half
    kb_ = 2
    while kb_ < c:
        ck = jnp.where(((r // (2 * kb_)) == (cc // (2 * kb_))) & ((r // kb_) != (cc // kb_)), lmat, 0.0)
        t = t - _mm(_mm(t, ck), t)
        kb_ *= 2
    sol = _mm(t, rhs)
    u0, w = sol[:, :DH_B], sol[:, DH_B:]

    aqk = (kq_k[n:] * decay).astype(bf16)
    qg = q * eg
    kg = k * jnp.exp(gtot_c - g_c)
    hr = lax.broadcasted_iota(jnp.int32, (n, H_B * DH_B), 0) // c
    hc = lax.broadcasted_iota(jnp.int32, (n, H_B * DH_B), 1) // DH_B
    blk = hr == hc
    spread = lambda t: jnp.where(blk, jnp.concatenate([t] * H_B, axis=1), 0.0).astype(bf16)
    blk_t = (lax.broadcasted_iota(jnp.int32, (H_B * DH_B, n), 0) // DH_B
             == lax.broadcasted_iota(jnp.int32, (H_B * DH_B, n), 1) // c)
    kgt = jnp.concatenate([kg.T] * H_B, axis=0)
    kgt = jnp.where(blk_t, kgt, 0.0).astype(bf16)
    gl = jnp.exp(jnp.sum(jnp.where(blk_t, la_r, 0.0), axis=1, keepdims=True))
    return u0, jnp.concatenate([spread(w), spread(qg)], axis=0), jnp.concatenate([aqk, kgt], axis=0), gl


def _gdn_chunk_step(prep, s_st):
    u0, wq_bd, ak, gl = prep
    c = GDN_CHUNK
    n = H_B * c
    ws_qs = _dot(wq_bd, s_st.astype(bf16))
    ub = (u0 - ws_qs[:n]).astype(bf16)
    au_ku = _dot(ak, ub)
    o = ws_qs[n:] + au_ku[:n]
    s_new = gl * s_st + au_ku[n:]
    return jnp.concatenate([o[h * c:(h + 1) * c] for h in range(H_B)], axis=1), s_new


def _gdn_kernel(xf_ref, xb_ref, grf_ref, grb_ref, prm_ref, s0_ref, of_ref, ob_ref, s_ref, *, chunks):
    @pl.when(pl.program_id(1) == 0)
    def _():
        s_ref[...] = s0_ref[...]

    c = GDN_CHUNK
    rows = lambda j: slice(j * c, (j + 1) * c)
    prep_f = [_gdn_chunk_prep(xf_ref[0, rows(j)], grf_ref[0, j, 0:2], prm_ref[0:2], False) for j in range(chunks)]
    prep_b = [_gdn_chunk_prep(xb_ref[0, rows(j)], grb_ref[0, j, 2:4], prm_ref[2:4], True) for j in range(chunks)]
    s_f = s_ref[0, 0]
    s_b = s_ref[0, 1]
    for j in range(chunks):
        o, s_f = _gdn_chunk_step(prep_f[j], s_f)
        of_ref[0, rows(j)] = o
    for j in reversed(range(chunks)):
        o, s_b = _gdn_chunk_step(prep_b[j], s_b)
        ob_ref[0, rows(j)] = o
    s_ref[0, 0] = s_f
    s_ref[0, 1] = s_b


def gdn_scan(x, gates_row, prm, s0, chunks=2):
    b, l, _ = x.shape
    c = GDN_CHUNK
    assert l % (c * chunks) == 0
    nb = l // (c * chunks)
    rows = c * chunks
    fwd = lambda bi, i: (bi, i, 0)
    bwd = lambda bi, i: (bi, nb - 1 - i, 0)
    st_spec = pl.BlockSpec((1, 2, H_B * DH_B, DH_B), lambda bi, i: (bi, 0, 0, 0))
    g_blk = (1, chunks, 4, H_B * c)
    return pl.pallas_call(
        functools.partial(_gdn_kernel, chunks=chunks),
        out_shape=[jax.ShapeDtypeStruct((b, l, B_W), f32), jax.ShapeDtypeStruct((b, l, B_W), f32),
                   jax.ShapeDtypeStruct(s0.shape, f32)],
        grid=(b, nb),
        in_specs=[
            pl.BlockSpec((1, rows, 3 * B_W), fwd), pl.BlockSpec((1, rows, 3 * B_W), bwd),
            pl.BlockSpec(g_blk, lambda bi, i: (bi, i, 0, 0)),
            pl.BlockSpec(g_blk, lambda bi, i: (bi, nb - 1 - i, 0, 0)),
            pl.BlockSpec((4, H_B * c), lambda bi, i: (0, 0)),
            st_spec,
        ],
        out_specs=[pl.BlockSpec((1, rows, B_W), fwd), pl.BlockSpec((1, rows, B_W), bwd), st_spec],
        compiler_params=_cparams(("parallel", "arbitrary")),
        name="gdn_scan",
    )(x, x, gates_row, gates_row, prm, s0)


def _na_kernel(q_ref, k_ref, v_ref, kc_ref, vc_ref, bias_ref, o_ref, *, rb, band, rows):
    i = pl.program_id(2)
    ub = jnp.clip(i * rb - WIN_R // 2, 0, rows - band)
    off = pl.multiple_of(ub * GRID_W, GRID_W)
    q = q_ref[0]
    kb = k_ref[0, pl.ds(off, band * GRID_W), :]
    vb = v_ref[0, pl.ds(off, band * GRID_W), :]
    s_lat = _dot_nt(q, kb) + bias_ref[0, 0]
    s_ctx = _dot_nt(q, kc_ref[0])
    m = jnp.maximum(jnp.max(s_lat, axis=-1, keepdims=True), jnp.max(s_ctx, axis=-1, keepdims=True))
    p_lat = jnp.exp(s_lat - m)
    p_ctx = jnp.exp(s_ctx - m)
    den = jnp.sum(p_lat, axis=-1, keepdims=True) + jnp.sum(p_ctx, axis=-1, keepdims=True)
    o = _dot(p_lat.astype(bf16), vb) + _dot(p_ctx.astype(bf16), vc_ref[0])
    o_ref[0] = (o * (1.0 / den)).astype(o_ref.dtype)


def _na_bias_index(rows, rb, band):
    wr = min(WIN_R, rows)
    nblk = rows // rb
    cols = np.arange(GRID_W)
    c_start = np.clip(cols - WIN_C // 2, 0, GRID_W - WIN_C)
    col_ok = (cols[None, :] >= c_start[:, None]) & (cols[None, :] < c_start[:, None] + WIN_C)
    dc_idx = np.clip(cols[None, :] - cols[:, None] + WIN_C - 1, 0, 2 * WIN_C - 2)

    def geometry(i):
        ub = int(np.clip(i * rb - WIN_R // 2, 0, rows - band))
        qr = i * rb + np.arange(rb)
        kr = ub + np.arange(band)
        r_start = np.clip(qr - wr // 2, 0, rows - wr)
        row_ok = (kr[None, :] >= r_start[:, None]) & (kr[None, :] < r_start[:, None] + wr)
        dr = np.clip(kr[None, :] - qr[:, None] + WIN_R - 1, 0, 2 * WIN_R - 2)
        ok = row_ok[:, None, :, None] & col_ok[None, :, None, :]
        return ok.reshape(rb * GRID_W, band * GRID_W), dr

    reps = [0, min(1, nblk - 1), nblk - 1]
    geo = [geometry(i) for i in reps]
    for i in range(1, nblk - 1):
        g = geometry(i)
        assert all(np.array_equal(a, b_) for a, b_ in zip(g, geo[1]))
    ok = np.stack([g[0] for g in geo])
    dr = np.stack([g[1] for g in geo])
    dr_onehot = (dr[..., None] == np.arange(2 * WIN_R - 1)).astype(np.float32)
    dc_onehot = (dc_idx[None] == np.arange(2 * WIN_C - 1)[:, None, None]).astype(np.float32)
    return ok, dr_onehot, dc_onehot


def _na_bias_table(rpb, rows, rb, band):
    ok, dr_onehot, dc_onehot = _na_bias_index(rows, rb, band)
    hi = lax.Precision.HIGHEST
    by_col = jnp.einsum('hrc,cqk->hrqk', rpb.astype(f32), dc_onehot, precision=hi)
    slabs = jnp.einsum('vabr,hrqk->vhaqbk', dr_onehot, by_col, precision=hi)
    h = rpb.shape[0]
    return jnp.where(ok[:, None], slabs.reshape(3, h, rb * GRID_W, band * GRID_W), NEG)


def neighbourhood_attention(q, k, v, k_ctx, v_ctx, rpb, rb=4):
    b, l, _ = q.shape
    lc = k_ctx.shape[1]
    rows = l // GRID_W
    band = rb + WIN_R - 1
    assert rows % rb == 0 and rows >= band and WIN_R <= rows
    nblk = rows // rb
    bias = _na_bias_table(rpb, rows, rb, band)
    tq = rb * GRID_W
    variant = lambda i: jnp.where(i == 0, 0, jnp.where(i == nblk - 1, 2, 1))
    head_blk = lambda n: pl.BlockSpec((1, n, LANES), lambda bi, h, i: (bi, 0, h))
    return pl.pallas_call(
        functools.partial(_na_kernel, rb=rb, band=band, rows=rows),
        out_shape=jax.ShapeDtypeStruct((b, l, H_C * DH_C), bf16),
        grid=(b, H_C, nblk),
        in_specs=[
            pl.BlockSpec((1, tq, LANES), lambda bi, h, i: (bi, i, h)),
            head_blk(l), head_blk(l), head_blk(lc), head_blk(lc),
            pl.BlockSpec((1, 1, tq, band * GRID_W), lambda bi, h, i: (variant(i), h, 0, 0)),
        ],
        out_specs=pl.BlockSpec((1, tq, LANES), lambda bi, h, i: (bi, i, h)),
        compiler_params=_cparams(("parallel", "parallel", "arbitrary")),
        name="na_attn",
    )(q, k, v, k_ctx, v_ctx, bias)


def _outproj_kernel(*refs, has_gdn):
    if has_gdn:
        x_ref, g_ref, a_ref, wa_ref, of_ref, ob_ref, gate_ref, gain_ref, wb_ref, o_ref = refs
    else:
        x_ref, g_ref, a_ref, wa_ref, o_ref = refs
    acc = _dot(a_ref[0], wa_ref[...])
    if has_gdn:
        o = of_ref[0] + ob_ref[0]
        gate = gate_ref[0]
        parts = []
        for h in range(H_B):
            sl = slice(h * DH_B, (h + 1) * DH_B)
            t = o[:, sl]
            y = t * lax.rsqrt(jnp.mean(t * t, axis=-1, keepdims=True) + EPS) * gain_ref[...]
            parts.append((y * _silu(gate[:, sl])).astype(bf16))
        acc = acc + _dot(jnp.concatenate(parts, axis=-1), wb_ref[...])
    o_ref[0] = x_ref[0] + g_ref[0] * acc


def out_project(x, g, a, wa, gdn=None, tm=512, name="outproj"):
    b, l, d = x.shape
    tm = min(tm, l)
    assert l % tm == 0
    bm = g.shape[0]
    mod_map = (lambda bi, i: (bi, 0, 0)) if bm == b else (lambda bi, i: (0, 0, 0))
    row = lambda n: pl.BlockSpec((1, tm, n), lambda bi, i: (bi, i, 0))
    full = lambda arr: pl.BlockSpec(arr.shape, lambda bi, i: (0, 0))
    in_specs = [row(d), pl.BlockSpec((1, 1, d), mod_map), row(a.shape[2]), full(wa)]
    args = [x, g, a, wa]
    if gdn is not None:
        o_f, o_b, gate, gain, wb = gdn
        in_specs += [row(B_W), row(B_W), row(B_W), full(gain), full(wb)]
        args += [o_f, o_b, gate, gain, wb]
    return pl.pallas_call(
        functools.partial(_outproj_kernel, has_gdn=gdn is not None),
        out_shape=jax.ShapeDtypeStruct((b, l, d), f32),
        grid=(b, l // tm),
        in_specs=in_specs,
        out_specs=row(d),
        compiler_params=_cparams(("parallel", "parallel")),
        name=name,
    )(*args)


def _ffn_kernel(x_ref, gain_ref, sc_ref, sh_ref, g_ref, wg_ref, wu_ref, wo_ref, o_ref, h_ref, acc_ref):
    j = pl.program_id(2)

    @pl.when(j == 0)
    def _():
        h_ref[...] = _norm_mod(x_ref[0], gain_ref[...], sc_ref[0], sh_ref[0]).astype(bf16)
        acc_ref[...] = jnp.zeros_like(acc_ref)

    tm = h_ref.shape[0]
    halves = [slice(r * (tm // 2), (r + 1) * (tm // 2)) for r in range(2)] if tm >= 512 else [slice(0, tm)]
    gate_up = []
    for rows in halves:
        h = h_ref[rows, :]
        gate_up.append((_dot(h, wg_ref[...]), _dot(h, wu_ref[...])))
    for rows, (gt, up) in zip(halves, gate_up):
        acc_ref[rows, :] += _dot((_silu(gt) * up).astype(bf16), wo_ref[...])

    @pl.when(j == pl.num_programs(2) - 1)
    def _():
        o_ref[0] = x_ref[0] + g_ref[0] * acc_ref[...]


def ffn(x, gain, sc, sh, g, w_in, w_out, tm=1024, tf=256, name="ffn"):
    b, l, d = x.shape
    ff = w_out.shape[0]
    tm = min(tm, l)
    assert l % tm == 0 and ff % tf == 0
    nf = ff // tf
    bm = sc.shape[0]
    mod_map = (lambda bi, i, j: (bi, 0, 0)) if bm == b else (lambda bi, i, j: (0, 0, 0))
    mod = pl.BlockSpec((1, 1, d), mod_map)
    return pl.pallas_call(
        _ffn_kernel,
        out_shape=jax.ShapeDtypeStruct((b, l, d), f32),
        grid=(b, l // tm, nf),
        in_specs=[
            pl.BlockSpec((1, tm, d), lambda bi, i, j: (bi, i, 0)),
            pl.BlockSpec((1, d), lambda bi, i, j: (0, 0)),
            mod, mod, mod,
            pl.BlockSpec((d, tf), lambda bi, i, j: (0, j)),
            pl.BlockSpec((d, tf), lambda bi, i, j: (0, j + nf)),
            pl.BlockSpec((tf, d), lambda bi, i, j: (j, 0)),
        ],
        out_specs=pl.BlockSpec((1, tm, d), lambda bi, i, j: (bi, i, 0)),
        scratch_shapes=[pltpu.VMEM((tm, d), bf16), pltpu.VMEM((tm, d), f32)],
        compiler_params=_cparams(("parallel", "parallel", "arbitrary")),
        name=name,
    )(x, gain, sc, sh, g, w_in, w_in, w_out)


def _rope_tables(l):
    n_freq = DH_A // 4
    inv = ROPE_BASE ** (-jnp.arange(n_freq, dtype=f32) / n_freq)
    t = jnp.arange(l)
    row = (t // GRID_W).astype(f32)
    col = (t % GRID_W).astype(f32)
    ang = jnp.concatenate([row[:, None] * inv, col[:, None] * inv], axis=-1)
    cos = jnp.repeat(jnp.cos(ang), 2, axis=-1)
    sin = jnp.repeat(jnp.sin(ang), 2, axis=-1)
    sign = jnp.tile(jnp.array([-1.0, 1.0], f32), DH_A // 2)
    return jnp.tile(cos, (1, 2)), jnp.tile(sin * sign, (1, 2))


def _pad_rows(a, rows):
    return jnp.zeros((rows,) + a.shape[1:], a.dtype).at[:a.shape[0]].set(a)


def kernel(x, c, ctx, c_ctx, ada_w, ada_b, norm_mix, norm_ffn, ffn_w_in, ffn_w_out, even_w_in, even_w_out,
           diff_qk_gain, diff_lambda, diff_subln, gdn_conv, gdn_a_log, gdn_dt_bias, gdn_norm, odd_w_in,
           odd_w_out, na_qk_gain, na_rpb):
    b, l, d = x.shape
    lc = ctx.shape[1]
    depth = ada_w.shape[0]
    c_all = _pad_rows(jnp.concatenate([c, c_ctx[None]], axis=0), 16)
    mods = ada_modulation(c_all, ada_w, ada_b)
    cos_t, sin_t = _rope_tables(l)
    ctx_flat = ctx.reshape(1, b * lc, d)

    for layer in range(depth):
        ctx_out = layer < depth - 1
        m_lat = [t[:, None, :] for t in jnp.split(mods[layer, :b], 6, axis=-1)]
        m_ctx = [t[:, None, :] for t in jnp.split(mods[layer, b:b + 1], 6, axis=-1)]
        sh_m, sc_m, g_m, sh_f, sc_f, g_f = m_lat
        csh_m, csc_m, cg_m, csh_f, csc_f, cg_f = m_ctx
        gain_m = norm_mix[layer][None]
        gain_f = norm_ffn[layer][None]
        if layer % 2 == 0:
            e = layer // 2
            lam_init = 0.8 - 0.6 * math.exp(-0.3 * layer)
            w_in = even_w_in[e].astype(bf16)
            cuts = [0, A_W, 2 * A_W, 3 * A_W, 3 * A_W + 3 * B_W, 3 * A_W + 4 * B_W]
            ws = [w_in[:, cuts[i]:cuts[i + 1]] for i in range(5)]
            w_gates = jnp.zeros((d, LANES), bf16).at[:, :4 * H_B].set(w_in[:, cuts[5]:])
            ws.append(w_gates)
            qk_gain = _pad_rows(jnp.tile(diff_qk_gain[e], (1, 2)), 8)
            dts = [bf16, bf16, bf16, f32, f32, f32]
            q_scale = DH_A ** -0.5 * math.log2(math.e)
            epi = lambda rope: [("submap", 0, rope, q_scale), ("submap", 1, rope, 1.0), ("plain",),
                                ("plain",), ("plain",), ("plain",)]
            qa, ka, va, qkv_b, g_b, gates = norm_mod_project(
                x, gain_m, sc_m, sh_m, ws, dts, epi(True), qk_gain, rope_tabs=(cos_t, sin_t), name="even_proj")
            qac, kac, vac, qkv_bc, g_bc, gates_c = norm_mod_project(
                ctx_flat, gain_m, csc_m, csh_m, ws, dts, epi(False), qk_gain, name="even_proj_ctx")
            unflat = lambda t: t.reshape(b, lc, t.shape[-1])
            qac, kac, vac, qkv_bc, g_bc, gates_c = map(unflat, (qac, kac, vac, qkv_bc, g_bc, gates_c))

            lv = diff_lambda[e]
            lam = jnp.exp(jnp.sum(lv[0] * lv[1])) - jnp.exp(jnp.sum(lv[2] * lv[3])) + lam_init
            gmax = jnp.max(jnp.abs(diff_qk_gain[e]), axis=-1)
            bound = 1.02 * DH_A * q_scale * gmax[0] * gmax[1]
            scal = jnp.stack([lam, bound]).astype(f32)
            subln = diff_subln[e][None]
            attn = functools.partial(diff_attention, scal, qa, kac, vac, ka, va, subln, lam_init)
            a_lat = lax.cond(2.0 * bound < 120.0,
                             lambda: attn(bounded=True, tk=1024, name="diff_attn"),
                             lambda: attn(bounded=False, name="diff_attn_online"))
            a_ctx = diff_attention(scal, qac, kac, vac, None, None, subln, lam_init, name="diff_attn_ctx")

            y_lat = gdn_short_conv(qkv_b, gdn_conv[e])
            y_ctx = gdn_short_conv(qkv_bc, gdn_conv[e])
            per_pos = lambda p: jnp.repeat(p, GDN_CHUNK)
            prm = jnp.stack([per_pos(gdn_a_log[e][0]), per_pos(gdn_dt_bias[e][0]),
                             per_pos(gdn_a_log[e][1]), per_pos(gdn_dt_bias[e][1])])

            def to_rows(g):
                t = g[:, :, :4 * H_B].reshape(g.shape[0], -1, GDN_CHUNK, 4, H_B).transpose(0, 1, 3, 4, 2)
                t = t.reshape(g.shape[0], -1, 4, H_B * GDN_CHUNK)
                return t[:, :, jnp.array([0, 2, 1, 3])]

            s0 = jnp.zeros((b, 2, H_B * DH_B, DH_B), f32)
            ocf, ocb, s_mid = gdn_scan(y_ctx, to_rows(gates_c), prm, s0)
            olf, olb, _ = gdn_scan(y_lat, to_rows(gates), prm, s_mid)

            w_out = even_w_out[e].astype(bf16)
            gdn_gain = gdn_norm[e][None]
            x = out_project(x, g_m, a_lat, w_out[:A_W], gdn=(olf, olb, g_b, gdn_gain, w_out[A_W:]),
                            name="even_out")
            ctx_flat = out_project(
                ctx_flat, cg_m, a_ctx.reshape(1, b * lc, A_W), w_out[:A_W],
                gdn=(ocf.reshape(1, b * lc, B_W), ocb.reshape(1, b * lc, B_W), g_bc.reshape(1, b * lc, B_W),
                     gdn_gain, w_out[A_W:]), name="even_out_ctx")
        else:
            od = layer // 2
            w_in = odd_w_in[od].astype(bf16)
            mix = H_C * DH_C
            ws = [w_in[:, :mix], w_in[:, mix:2 * mix], w_in[:, 2 * mix:]]
            qk_gain = _pad_rows(na_qk_gain[od], 8)
            q, k, v = norm_mod_project(
                x, gain_m, sc_m, sh_m, ws, [bf16] * 3,
                [("headnorm", 0, DH_C ** -0.5), ("headnorm", 1, 1.0), ("plain",)], qk_gain, name="odd_proj")
            kc, vc = norm_mod_project(
                ctx_flat, gain_m, csc_m, csh_m, ws[1:], [bf16] * 2,
                [("headnorm", 1, 1.0), ("plain",)], qk_gain, name="odd_proj_ctx")
            kc, vc = kc.reshape(b, lc, mix), vc.reshape(b, lc, mix)
            o = neighbourhood_attention(q, k, v, kc, vc, na_rpb[od])
            x = out_project(x, g_m, o, odd_w_out[od].astype(bf16), name="odd_out")
            if ctx_out:
                raise NotImplementedError("context output of a neighbourhood layer is not needed at depth 2")

        wf_in = ffn_w_in[layer].astype(bf16)
        wf_out = ffn_w_out[layer].astype(bf16)
        x = ffn(x, gain_f, sc_f, sh_f, g_f, wf_in, wf_out, name="ffn")
        if ctx_out:
            ctx_flat = ffn(ctx_flat, gain_f, csc_f, csh_f, cg_f, wf_in, wf_out, name="ffn_ctx")
    return x
```

```python
import functools
import math

import numpy as np
import jax
import jax.numpy as jnp
from jax import lax
from jax.experimental import pallas as pl
from jax.experimental.pallas import tpu as pltpu

f32 = jnp.float32
bf16 = jnp.bfloat16

EPS = 1e-6
ROPE_BASE = 10000.0
GRID_W = 64
H_A = 4
DH_A = 64
DV_A = 2 * DH_A
A_W = H_A * DV_A
H_B = 4
DH_B = 128
B_W = H_B * DH_B
GDN_CHUNK = 64
CONV_K = 5
H_C = 8
DH_C = 128
WIN_R = 8
WIN_C = 16

LANES = 128
VMEM_LIMIT = 56 * 1024 * 1024
NEG = -1e30


def _cparams(sem):
    return pltpu.CompilerParams(dimension_semantics=sem, vmem_limit_bytes=VMEM_LIMIT)


def _silu(x):
    return x * (1.0 / (1.0 + jnp.exp(-x)))


def _dot(a, b):
    return jnp.dot(a, b, preferred_element_type=f32)


def _dot_nt(a, b):
    return lax.dot_general(a, b, (((1,), (1,)), ((), ())), preferred_element_type=f32)


def _split_bf16(x):
    hi = x.astype(bf16)
    lo = (x - hi.astype(f32)).astype(bf16)
    return hi, lo


def _dot3(a, b):
    ah, al = _split_bf16(a)
    bh, bl = _split_bf16(b)
    return _dot(ah, bh) + (_dot(ah, bl) + _dot(al, bh))


def _mm(a, b):
    return _dot(a.astype(bf16), b.astype(bf16))


def _dot3_nt(a, b):
    ah, al = _split_bf16(a)
    bh, bl = _split_bf16(b)
    return _dot_nt(ah, bh) + (_dot_nt(ah, bl) + _dot_nt(al, bh))


def _ada_kernel(c_ref, w_ref, b_ref, o_ref):
    sc = _silu(c_ref[...])
    o_ref[0] = _dot3(sc, w_ref[0]) + b_ref[0]


def ada_modulation(c_all, ada_w, ada_b):
    depth, d, n = ada_w.shape
    rows = c_all.shape[0]
    tn = 1536
    return pl.pallas_call(
        _ada_kernel,
        out_shape=jax.ShapeDtypeStruct((depth, rows, n), f32),
        grid=(depth, n // tn),
        in_specs=[
            pl.BlockSpec((rows, d), lambda l, j: (0, 0)),
            pl.BlockSpec((1, d, tn), lambda l, j: (l, 0, j)),
            pl.BlockSpec((1, 1, tn), lambda l, j: (l, 0, j)),
        ],
        out_specs=pl.BlockSpec((1, rows, tn), lambda l, j: (l, 0, j)),
        compiler_params=_cparams(("parallel", "parallel")),
        name="ada_modulation",
    )(c_all, ada_w, ada_b.reshape(depth, 1, n))


def _norm_mod(x, gain, sc, sh):
    y = x * lax.rsqrt(jnp.mean(x * x, axis=-1, keepdims=True) + EPS)
    return (y * gain) * (1.0 + sc) + sh


def _swap_pairs(x):
    lane = lax.broadcasted_iota(jnp.int32, x.shape, x.ndim - 1)
    nxt = pltpu.roll(x, x.shape[-1] - 1, x.ndim - 1)
    prv = pltpu.roll(x, 1, x.ndim - 1)
    return jnp.where(lane % 2 == 0, nxt, prv)


def _submap_norm_rope(t, gain, cos, sin, scale):
    lane = lax.broadcasted_iota(jnp.int32, t.shape, 1)
    lo = lane < DH_A
    sq = t * t
    s_lo = jnp.sum(jnp.where(lo, sq, 0.0), axis=-1, keepdims=True)
    s_hi = jnp.sum(jnp.where(lo, 0.0, sq), axis=-1, keepdims=True)
    r = jnp.where(lo, lax.rsqrt(s_lo * (1.0 / DH_A) + EPS), lax.rsqrt(s_hi * (1.0 / DH_A) + EPS))
    y = t * r * gain
    if cos is not None:
        y = y * cos + _swap_pairs(y) * sin
    if scale != 1.0:
        y = y * scale
    return y


def _head_norm(t, gain, scale):
    y = t * lax.rsqrt(jnp.mean(t * t, axis=-1, keepdims=True) + EPS) * gain
    if scale != 1.0:
        y = y * scale
    return y


def _proj_kernel(*refs, epilogues, rope):
    n_out = len(epilogues)
    x_ref, gain_ref, sc_ref, sh_ref = refs[:4]
    pos = 4
    if rope:
        cos_ref, sin_ref = refs[pos:pos + 2]
        pos += 2
    qkg_ref = refs[pos]
    pos += 1
    w_refs = refs[pos:pos + n_out]
    o_refs = refs[pos + n_out:pos + 2 * n_out]

    h = _norm_mod(x_ref[0], gain_ref[...], sc_ref[0], sh_ref[0]).astype(bf16)
    for w_ref, o_ref, epi in zip(w_refs, o_refs, epilogues):
        acc = _dot(h, w_ref[...])
        kind = epi[0]
        if kind == "plain":
            o_ref[0] = acc.astype(o_ref.dtype)
        elif kind == "submap":
            _, grow, use_rope, scale = epi
            gain = qkg_ref[grow:grow + 1, :]
            for hd in range(acc.shape[1] // LANES):
                t = acc[:, hd * LANES:(hd + 1) * LANES]
                cs = (cos_ref[...], sin_ref[...]) if (rope and use_rope) else (None, None)
                y = _submap_norm_rope(t, gain, cs[0], cs[1], scale)
                o_ref[0, :, hd * LANES:(hd + 1) * LANES] = y.astype(o_ref.dtype)
        elif kind == "headnorm":
            _, grow, scale = epi
            gain = qkg_ref[grow:grow + 1, :]
            for hd in range(acc.shape[1] // LANES):
                t = acc[:, hd * LANES:(hd + 1) * LANES]
                o_ref[0, :, hd * LANES:(hd + 1) * LANES] = _head_norm(t, gain, scale).astype(o_ref.dtype)
        else:
            raise ValueError(kind)


def norm_mod_project(x, gain, sc, sh, weights, out_dtypes, epilogues, qk_gain, rope_tabs=None,
                     tm=512, name="proj"):
    b, l, d = x.shape
    tm = min(tm, l)
    assert l % tm == 0
    bm = sc.shape[0]
    mod_map = (lambda bi, i: (bi, 0, 0)) if bm == b else (lambda bi, i: (0, 0, 0))
    rope = rope_tabs is not None
    in_specs = [
        pl.BlockSpec((1, tm, d), lambda bi, i: (bi, i, 0)),
        pl.BlockSpec((1, d), lambda bi, i: (0, 0)),
        pl.BlockSpec((1, 1, d), mod_map),
        pl.BlockSpec((1, 1, d), mod_map),
    ]
    args = [x, gain, sc, sh]
    if rope:
        in_specs += [pl.BlockSpec((tm, LANES), lambda bi, i: (i, 0))] * 2
        args += list(rope_tabs)
    in_specs.append(pl.BlockSpec(qk_gain.shape, lambda bi, i: (0, 0)))
    args.append(qk_gain)
    for w in weights:
        in_specs.append(pl.BlockSpec(w.shape, lambda bi, i: (0, 0)))
        args.append(w)
    out_shape = [jax.ShapeDtypeStruct((b, l, w.shape[1]), dt) for w, dt in zip(weights, out_dtypes)]
    out_specs = [pl.BlockSpec((1, tm, w.shape[1]), lambda bi, i: (bi, i, 0)) for w in weights]
    return pl.pallas_call(
        functools.partial(_proj_kernel, epilogues=tuple(epilogues), rope=rope),
        out_shape=out_shape,
        grid=(b, l // tm),
        in_specs=in_specs,
        out_specs=out_specs,
        compiler_params=_cparams(("parallel", "parallel")),
        name=name,
    )(*args)


def _diff_attn_kernel(*refs, has_lat, tk, lam_init, bounded):
    if has_lat:
        sc_ref, q_ref, kc_ref, vc_ref, k_ref, v_ref, subln_ref, o_ref = refs
    else:
        sc_ref, q_ref, kc_ref, vc_ref, subln_ref, o_ref = refs
    q = q_ref[0]
    tq = q.shape[0]
    lane = lax.broadcasted_iota(jnp.int32, q.shape, 1)
    zero = jnp.zeros_like(q)
    q1 = jnp.where(lane < DH_A, q, zero)
    q2 = jnp.where(lane < DH_A, zero, q)

    def kv_loop(step, carry):
        carry = step(kc_ref[0], vc_ref[0], carry)
        if has_lat:
            def body(j, c):
                off = pl.multiple_of(j * tk, tk)
                return step(k_ref[0, pl.ds(off, tk), :], v_ref[0, pl.ds(off, tk), :], c)

            carry = lax.fori_loop(0, k_ref.shape[1] // tk, body, carry)
        return carry

    if bounded:
        bound = sc_ref[1]

        def step(kblk, vblk, carry):
            ones_col = jnp.where(lax.broadcasted_iota(jnp.int32, vblk.shape, 1) == 0, 1.0, 0.0).astype(bf16)
            v_aug = jnp.concatenate([vblk, ones_col], axis=1)
            a1, a2 = carry
            a1 = a1 + _dot(jnp.exp2(_dot_nt(q1, kblk) - bound).astype(bf16), v_aug)
            a2 = a2 + _dot(jnp.exp2(_dot_nt(q2, kblk) - bound).astype(bf16), v_aug)
            return a1, a2

        ai = jnp.zeros((tq, 2 * DV_A), f32)
        a1, a2 = kv_loop(step, (ai, ai))
        l1 = jnp.sum(a1[:, DV_A:], axis=-1, keepdims=True)
        l2 = jnp.sum(a2[:, DV_A:], axis=-1, keepdims=True)
        a1, a2 = a1[:, :DV_A], a2[:, :DV_A]
    else:
        def one_map(qm, kblk, vblk, m, l, acc):
            s = _dot_nt(qm, kblk)
            m_new = jnp.maximum(m, jnp.max(s, axis=-1, keepdims=True))
            p = jnp.exp2(s - m_new)
            a = jnp.exp2(m - m_new)
            l = a * l + jnp.sum(p, axis=-1, keepdims=True)
            acc = a * acc + _dot(p.astype(bf16), vblk)
            return m_new, l, acc

        def step(kblk, vblk, carry):
            m1, l1, a1, m2, l2, a2 = carry
            m1, l1, a1 = one_map(q1, kblk, vblk, m1, l1, a1)
            m2, l2, a2 = one_map(q2, kblk, vblk, m2, l2, a2)
            return m1, l1, a1, m2, l2, a2

        mi = jnp.full((tq, 1), NEG, f32)
        li = jnp.zeros((tq, 1), f32)
        ai = jnp.zeros((tq, DV_A), f32)
        _, l1, a1, _, l2, a2 = kv_loop(step, (mi, li, ai, mi, li, ai))
    lam = sc_ref[0]
    o = a1 * (1.0 / l1) - lam * (a2 * (1.0 / l2))
    y = o * lax.rsqrt(jnp.mean(o * o, axis=-1, keepdims=True) + EPS) * subln_ref[...]
    o_ref[0] = (y * (1.0 - lam_init)).astype(o_ref.dtype)


def diff_attention(lam, q, k_ctx, v_ctx, k_lat, v_lat, subln, lam_init, bounded=False, tq=512, tk=512,
                   name="diff_attn"):
    b, lq, _ = q.shape
    lc = k_ctx.shape[1]
    has_lat = k_lat is not None
    tq = min(tq, lq)
    assert lq % tq == 0
    head_blk = lambda rows: pl.BlockSpec((1, rows, LANES), lambda bi, h, i: (bi, 0, h))
    in_specs = [
        pl.BlockSpec(memory_space=pltpu.SMEM),
        pl.BlockSpec((1, tq, LANES), lambda bi, h, i: (bi, i, h)),
        head_blk(lc), head_blk(lc),
    ]
    args = [lam, q, k_ctx, v_ctx]
    if has_lat:
        ll = k_lat.shape[1]
        tk = min(tk, ll)
        assert ll % tk == 0
        in_specs += [head_blk(ll), head_blk(ll)]
        args += [k_lat, v_lat]
    in_specs.append(pl.BlockSpec((1, LANES), lambda bi, h, i: (0, 0)))
    args.append(subln)
    return pl.pallas_call(
        functools.partial(_diff_attn_kernel, has_lat=has_lat, tk=tk, lam_init=lam_init, bounded=bounded),
        out_shape=jax.ShapeDtypeStruct((b, lq, A_W), bf16),
        grid=(b, H_A, lq // tq),
        in_specs=in_specs,
        out_specs=pl.BlockSpec((1, tq, LANES), lambda bi, h, i: (bi, i, h)),
        compiler_params=_cparams(("parallel", "parallel", "arbitrary")),
        name=name,
    )(*args)


def _gdn_conv_kernel(x_ref, w_ref, o_ref, pad_ref, *, rows_per_step):
    l = x_ref.shape[1]
    half = CONV_K // 2
    halo = 8
    zeros = jnp.zeros((halo, LANES), f32)
    pad_ref[0:halo, :] = zeros
    pad_ref[halo + l:halo + l + halo, :] = zeros
    pad_ref[halo:halo + l, :] = x_ref[0]
    kind = pl.program_id(1) // H_B
    w = w_ref[...]
    r = rows_per_step

    def body(i, _):
        t0 = pl.multiple_of(i * r, r)
        win = pad_ref[pl.ds(t0, r + 2 * halo), :]
        y = jnp.zeros((r, LANES), f32)
        for j in range(CONV_K):
            s = halo - half + j
            y = y + win[s:s + r, :] * w[j:j + 1, :]
        y = _silu(y)
        nrm = lax.rsqrt(jnp.sum(y * y, axis=-1, keepdims=True) + EPS)
        nrm = jnp.where(kind == 0, nrm * (DH_B ** -0.5), nrm)
        y = jnp.where(kind == 2, y, y * nrm)
        o_ref[0, pl.ds(t0, r), :] = y
        return 0

    lax.fori_loop(0, l // r, body, 0)


def gdn_short_conv(qkv, conv_w):
    b, l, c = qkv.shape
    r = min(512, l)
    assert l % r == 0
    wpad = jnp.zeros((8, c), f32).at[:CONV_K].set(conv_w)
    return pl.pallas_call(
        functools.partial(_gdn_conv_kernel, rows_per_step=r),
        out_shape=jax.ShapeDtypeStruct((b, l, c), f32),
        grid=(b, c // LANES),
        in_specs=[
            pl.BlockSpec((1, l, LANES), lambda bi, j: (bi, 0, j)),
            pl.BlockSpec((8, LANES), lambda bi, j: (0, j)),
        ],
        out_specs=pl.BlockSpec((1, l, LANES), lambda bi, j: (bi, 0, j)),
        scratch_shapes=[pltpu.VMEM((l + 16, LANES), f32)],
        compiler_params=_cparams(("parallel", "parallel")),
        name="gdn_conv",
    )(qkv, wpad)


def _softplus(x):
    return jnp.maximum(x, 0.0) + jnp.log1p(jnp.exp(-jnp.abs(x)))


def _sigmoid(x):
    return 1.0 / (1.0 + jnp.exp(-x))


def _stack_heads(x, base):
    return jnp.concatenate([x[:, base + h * DH_B:base + (h + 1) * DH_B] for h in range(H_B)], axis=0)


def _bmm(a, b):
    return jnp.einsum('bij,bjk->bik', a.astype(bf16), b.astype(bf16), preferred_element_type=f32)


def _bmm_nt(a, b):
    return jnp.einsum('bik,bjk->bij', a.astype(bf16), b.astype(bf16), preferred_element_type=f32)


def _gdn_chunk_prep(x, graw, prm, n_fwd):
    c = GDN_CHUNK
    n = H_B * c
    nb = x.shape[0]
    stack = lambda base: jnp.concatenate(
        [x[:, :, base + h * DH_B:base + (h + 1) * DH_B] for h in range(H_B)], axis=1)
    q, k, v = stack(0), stack(B_W), stack(2 * B_W)
    shp = (nb, n, n)
    chain = lax.broadcasted_iota(jnp.int32, shp, 0)
    r = lax.broadcasted_iota(jnp.int32, shp, 1)
    cc = lax.broadcasted_iota(jnp.int32, shp, 2)
    same = (r // c) == (cc // c)
    eye = r == cc
    ahead = jnp.where(chain < n_fwd, r - cc, cc - r)
    far = jnp.int32(4 * n)
    strict = jnp.where(same, ahead, -far) > 0
    incl = jnp.where(same, ahead, -far) >= 0
    incl_t = jnp.where(same, ahead, far) <= 0

    beta_r = _sigmoid(graw[:, 0:1, :])
    la_r = -jnp.exp(prm[:, 0:1, :]) * _softplus(graw[:, 1:2, :] + prm[:, 1:2, :])
    to_col = lambda row: jnp.sum(jnp.where(eye, row, 0.0), axis=2, keepdims=True)
    beta_c = to_col(beta_r)
    la_c = to_col(la_r)
    g_c = jnp.sum(jnp.where(incl, la_r, 0.0), axis=2, keepdims=True)
    g_r = jnp.sum(jnp.where(incl_t, la_c, 0.0), axis=1, keepdims=True)
    gtot_c = jnp.sum(jnp.where(same, la_r, 0.0), axis=2, keepdims=True)
    decay = jnp.where(incl, jnp.exp(jnp.where(incl, g_c - g_r, 0.0)), 0.0)

    kb = k.astype(bf16)
    kq_k = _bmm_nt(jnp.concatenate([kb, q.astype(bf16)], axis=1), kb)
    lmat = jnp.where(strict, beta_c * decay * kq_k[:, :n], 0.0)
    eg = jnp.exp(g_c)
    rhs = jnp.concatenate([beta_c * v, (beta_c * eg) * k], axis=-1)
    half = jnp.where((r // 2) == (cc // 2), lmat, 0.0)
    t = jnp.where(eye, 1.0, 0.0) - half
    kb_ = 2
    while kb_ < c:
        ck = jnp.where(((r // (2 * kb_)) == (cc // (2 * kb_))) & ((r // kb_) != (cc // kb_)), lmat, 0.0)
        t = t - _bmm(_bmm(t, ck), t)
        kb_ *= 2
    sol = _bmm(t, rhs)
    u0, w = sol[:, :, :DH_B], sol[:, :, DH_B:]

    aqk = (kq_k[:, n:] * decay).astype(bf16)
    qg = q * eg
    kg = k * jnp.exp(gtot_c - g_c)
    wide = (nb, n, H_B * DH_B)
    blk = (lax.broadcasted_iota(jnp.int32, wide, 1) // c) == (lax.broadcasted_iota(jnp.int32, wide, 2) // DH_B)
    spread = lambda m: jnp.where(blk, jnp.concatenate([m] * H_B, axis=2), 0.0).astype(bf16)
    tall = (nb, H_B * DH_B, n)
    blk_t = (lax.broadcasted_iota(jnp.int32, tall, 1) // DH_B) == (lax.broadcasted_iota(jnp.int32, tall, 2) // c)
    kg_t = jnp.stack([kg[i].T for i in range(nb)])
    kgt = jnp.where(blk_t, jnp.concatenate([kg_t] * H_B, axis=1), 0.0).astype(bf16)
    gl = jnp.exp(jnp.sum(jnp.where(blk_t, la_r, 0.0), axis=2, keepdims=True))
    return u0, jnp.concatenate([spread(w), spread(qg)], axis=1), jnp.concatenate([aqk, kgt], axis=1), gl


def _gdn_chunk_step(prep, idx, s_st):
    pick = lambda t: jnp.stack([t[i] for i in idx])
    u0, wq_bd, ak, gl = (pick(t) for t in prep)
    n = H_B * GDN_CHUNK
    ws_qs = _bmm(wq_bd, s_st)
    ub = (u0 - ws_qs[:, :n]).astype(bf16)
    au_ku = _bmm(ak, ub)
    o = ws_qs[:, n:] + au_ku[:, :n]
    return o, gl * s_st + au_ku[:, n:]


def _gdn_kernel(xf_ref, xb_ref, grf_ref, grb_ref, prm_ref, s0_ref, of_ref, ob_ref, s_ref, *, chunks):
    @pl.when(pl.program_id(1) == 0)
    def _():
        s_ref[...] = s0_ref[...]

    c = GDN_CHUNK
    rows = lambda j: slice(j * c, (j + 1) * c)
    x = jnp.stack([xf_ref[0, rows(j)] for j in range(chunks)] + [xb_ref[0, rows(j)] for j in range(chunks)])
    graw = jnp.stack([grf_ref[0, j, 0:2] for j in range(chunks)] + [grb_ref[0, j, 2:4] for j in range(chunks)])
    prm = jnp.stack([prm_ref[0:2]] * chunks + [prm_ref[2:4]] * chunks)
    prep = _gdn_chunk_prep(x, graw, prm, chunks)
    s = s_ref[0]
    heads_to_lanes = lambda o: jnp.concatenate([o[h * c:(h + 1) * c] for h in range(H_B)], axis=1)
    for j in range(chunks):
        jb = chunks - 1 - j
        o, s = _gdn_chunk_step(prep, (j, chunks + jb), s)
        of_ref[0, rows(j)] = heads_to_lanes(o[0])
        ob_ref[0, rows(jb)] = heads_to_lanes(o[1])
    s_ref[0] = s


def gdn_scan(x, gates_row, prm, s0, chunks=2):
    b, l, _ = x.shape
    c = GDN_CHUNK
    assert l % (c * chunks) == 0
    nb = l // (c * chunks)
    rows = c * chunks
    fwd = lambda bi, i: (bi, i, 0)
    bwd = lambda bi, i: (bi, nb - 1 - i, 0)
    st_spec = pl.BlockSpec((1, 2, H_B * DH_B, DH_B), lambda bi, i: (bi, 0, 0, 0))
    g_blk = (1, chunks, 4, H_B * c)
    return pl.pallas_call(
        functools.partial(_gdn_kernel, chunks=chunks),
        out_shape=[jax.ShapeDtypeStruct((b, l, B_W), f32), jax.ShapeDtypeStruct((b, l, B_W), f32),
                   jax.ShapeDtypeStruct(s0.shape, f32)],
        grid=(b, nb),
        in_specs=[
            pl.BlockSpec((1, rows, 3 * B_W), fwd), pl.BlockSpec((1, rows, 3 * B_W), bwd),
            pl.BlockSpec(g_blk, lambda bi, i: (bi, i, 0, 0)),
            pl.BlockSpec(g_blk, lambda bi, i: (bi, nb - 1 - i, 0, 0)),
            pl.BlockSpec((4, H_B * c), lambda bi, i: (0, 0)),
            st_spec,
        ],
        out_specs=[pl.BlockSpec((1, rows, B_W), fwd), pl.BlockSpec((1, rows, B_W), bwd), st_spec],
        compiler_params=_cparams(("parallel", "arbitrary")),
        name="gdn_scan",
    )(x, x, gates_row, gates_row, prm, s0)


def _na_kernel(q_ref, k_ref, v_ref, kc_ref, vc_ref, bias_ref, o_ref, *, rb, band, rows, hp):
    i = pl.program_id(2)
    ub = jnp.clip(i * rb - WIN_R // 2, 0, rows - band)
    off = pl.multiple_of(ub * GRID_W, GRID_W)
    heads = lambda t: jnp.stack([t[:, h * DH_C:(h + 1) * DH_C] for h in range(hp)])
    q = heads(q_ref[0])
    kb = heads(k_ref[0, pl.ds(off, band * GRID_W), :])
    vb = heads(v_ref[0, pl.ds(off, band * GRID_W), :])
    kc = heads(kc_ref[0])
    vc = heads(vc_ref[0])
    bqk = lambda a, b_: jnp.einsum('hqd,hkd->hqk', a, b_, preferred_element_type=f32)
    bpv = lambda a, b_: jnp.einsum('hqk,hkd->hqd', a, b_, preferred_element_type=f32)
    s_lat = bqk(q, kb) + bias_ref[0]
    s_ctx = bqk(q, kc)
    m = jnp.maximum(jnp.max(s_lat, axis=-1, keepdims=True), jnp.max(s_ctx, axis=-1, keepdims=True))
    p_lat = jnp.exp(s_lat - m)
    p_ctx = jnp.exp(s_ctx - m)
    den = jnp.sum(p_lat, axis=-1, keepdims=True) + jnp.sum(p_ctx, axis=-1, keepdims=True)
    o = (bpv(p_lat.astype(bf16), vb) + bpv(p_ctx.astype(bf16), vc)) * (1.0 / den)
    o_ref[0] = jnp.concatenate([o[h] for h in range(hp)], axis=-1).astype(o_ref.dtype)


def _na_bias_index(rows, rb, band):
    wr = min(WIN_R, rows)
    nblk = rows // rb
    cols = np.arange(GRID_W)
    c_start = np.clip(cols - WIN_C // 2, 0, GRID_W - WIN_C)
    col_ok = (cols[None, :] >= c_start[:, None]) & (cols[None, :] < c_start[:, None] + WIN_C)
    dc_idx = np.clip(cols[None, :] - cols[:, None] + WIN_C - 1, 0, 2 * WIN_C - 2)

    def geometry(i):
        ub = int(np.clip(i * rb - WIN_R // 2, 0, rows - band))
        qr = i * rb + np.arange(rb)
        kr = ub + np.arange(band)
        r_start = np.clip(qr - wr // 2, 0, rows - wr)
        row_ok = (kr[None, :] >= r_start[:, None]) & (kr[None, :] < r_start[:, None] + wr)
        dr = np.clip(kr[None, :] - qr[:, None] + WIN_R - 1, 0, 2 * WIN_R - 2)
        ok = row_ok[:, None, :, None] & col_ok[None, :, None, :]
        return ok.reshape(rb * GRID_W, band * GRID_W), dr

    reps = [0, min(1, nblk - 1), nblk - 1]
    geo = [geometry(i) for i in reps]
    for i in range(1, nblk - 1):
        g = geometry(i)
        assert all(np.array_equal(a, b_) for a, b_ in zip(g, geo[1]))
    ok = np.stack([g[0] for g in geo])
    dr = np.stack([g[1] for g in geo])
    dr_onehot = (dr[..., None] == np.arange(2 * WIN_R - 1)).astype(np.float32)
    dc_onehot = (dc_idx[None] == np.arange(2 * WIN_C - 1)[:, None, None]).astype(np.float32)
    return ok, dr_onehot, dc_onehot


def _na_bias_table(rpb, rows, rb, band):
    ok, dr_onehot, dc_onehot = _na_bias_index(rows, rb, band)
    hi = lax.Precision.HIGHEST
    by_col = jnp.einsum('hrc,cqk->hrqk', rpb.astype(f32), dc_onehot, precision=hi)
    slabs = jnp.einsum('vabr,hrqk->vhaqbk', dr_onehot, by_col, precision=hi)
    h = rpb.shape[0]
    return jnp.where(ok[:, None], slabs.reshape(3, h, rb * GRID_W, band * GRID_W), NEG)


def neighbourhood_attention(q, k, v, k_ctx, v_ctx, rpb, rb=4):
    b, l, _ = q.shape
    lc = k_ctx.shape[1]
    rows = l // GRID_W
    band = rb + WIN_R - 1
    assert rows % rb == 0 and rows >= band and WIN_R <= rows
    nblk = rows // rb
    bias = _na_bias_table(rpb, rows, rb, band)
    tq = rb * GRID_W
    variant = lambda i: jnp.where(i == 0, 0, jnp.where(i == nblk - 1, 2, 1))
    hp = 4
    assert H_C % hp == 0
    hw = hp * DH_C
    head_blk = lambda n: pl.BlockSpec((1, n, hw), lambda bi, h, i: (bi, 0, h))
    return pl.pallas_call(
        functools.partial(_na_kernel, rb=rb, band=band, rows=rows, hp=hp),
        out_shape=jax.ShapeDtypeStruct((b, l, H_C * DH_C), bf16),
        grid=(b, H_C // hp, nblk),
        in_specs=[
            pl.BlockSpec((1, tq, hw), lambda bi, h, i: (bi, i, h)),
            head_blk(l), head_blk(l), head_blk(lc), head_blk(lc),
            pl.BlockSpec((1, hp, tq, band * GRID_W), lambda bi, h, i: (variant(i), h, 0, 0)),
        ],
        out_specs=pl.BlockSpec((1, tq, hw), lambda bi, h, i: (bi, i, h)),
        compiler_params=_cparams(("parallel", "parallel", "arbitrary")),
        name="na_attn",
    )(q, k, v, k_ctx, v_ctx, bias)


def _outproj_kernel(*refs, has_gdn):
    if has_gdn:
        x_ref, g_ref, a_ref, wa_ref, of_ref, ob_ref, gate_ref, gain_ref, wb_ref, o_ref = refs
    else:
        x_ref, g_ref, a_ref, wa_ref, o_ref = refs
    acc = _dot(a_ref[0], wa_ref[...])
    if has_gdn:
        o = of_ref[0] + ob_ref[0]
        gate = gate_ref[0]
        parts = []
        for h in range(H_B):
            sl = slice(h * DH_B, (h + 1) * DH_B)
            t = o[:, sl]
            y = t * lax.rsqrt(jnp.mean(t * t, axis=-1, keepdims=True) + EPS) * gain_ref[...]
            parts.append((y * _silu(gate[:, sl])).astype(bf16))
        acc = acc + _dot(jnp.concatenate(parts, axis=-1), wb_ref[...])
    o_ref[0] = x_ref[0] + g_ref[0] * acc


def out_project(x, g, a, wa, gdn=None, tm=512, name="outproj"):
    b, l, d = x.shape
    tm = min(tm, l)
    assert l % tm == 0
    bm = g.shape[0]
    mod_map = (lambda bi, i: (bi, 0, 0)) if bm == b else (lambda bi, i: (0, 0, 0))
    row = lambda n: pl.BlockSpec((1, tm, n), lambda bi, i: (bi, i, 0))
    full = lambda arr: pl.BlockSpec(arr.shape, lambda bi, i: (0, 0))
    in_specs = [row(d), pl.BlockSpec((1, 1, d), mod_map), row(a.shape[2]), full(wa)]
    args = [x, g, a, wa]
    if gdn is not None:
        o_f, o_b, gate, gain, wb = gdn
        in_specs += [row(B_W), row(B_W), row(B_W), full(gain), full(wb)]
        args += [o_f, o_b, gate, gain, wb]
    return pl.pallas_call(
        functools.partial(_outproj_kernel, has_gdn=gdn is not None),
        out_shape=jax.ShapeDtypeStruct((b, l, d), f32),
        grid=(b, l // tm),
        in_specs=in_specs,
        out_specs=row(d),
        compiler_params=_cparams(("parallel", "parallel")),
        name=name,
    )(*args)


def _ffn_kernel(x_ref, gain_ref, sc_ref, sh_ref, g_ref, wg_ref, wu_ref, wo_ref, o_ref, h_ref, acc_ref):
    j = pl.program_id(2)

    @pl.when(j == 0)
    def _():
        h_ref[...] = _norm_mod(x_ref[0], gain_ref[...], sc_ref[0], sh_ref[0]).astype(bf16)
        acc_ref[...] = jnp.zeros_like(acc_ref)

    tm = h_ref.shape[0]
    halves = [slice(r * (tm // 2), (r + 1) * (tm // 2)) for r in range(2)] if tm >= 512 else [slice(0, tm)]
    gate_up = []
    for rows in halves:
        h = h_ref[rows, :]
        gate_up.append((_dot(h, wg_ref[...]), _dot(h, wu_ref[...])))
    for rows, (gt, up) in zip(halves, gate_up):
        acc_ref[rows, :] += _dot((_silu(gt) * up).astype(bf16), wo_ref[...])

    @pl.when(j == pl.num_programs(2) - 1)
    def _():
        o_ref[0] = x_ref[0] + g_ref[0] * acc_ref[...]


def ffn(x, gain, sc, sh, g, w_in, w_out, tm=1024, tf=256, name="ffn"):
    b, l, d = x.shape
    ff = w_out.shape[0]
    tm = min(tm, l)
    assert l % tm == 0 and ff % tf == 0
    nf = ff // tf
    bm = sc.shape[0]
    mod_map = (lambda bi, i, j: (bi, 0, 0)) if bm == b else (lambda bi, i, j: (0, 0, 0))
    mod = pl.BlockSpec((1, 1, d), mod_map)
    return pl.pallas_call(
        _ffn_kernel,
        out_shape=jax.ShapeDtypeStruct((b, l, d), f32),
        grid=(b, l // tm, nf),
        in_specs=[
            pl.BlockSpec((1, tm, d), lambda bi, i, j: (bi, i, 0)),
            pl.BlockSpec((1, d), lambda bi, i, j: (0, 0)),
            mod, mod, mod,
            pl.BlockSpec((d, tf), lambda bi, i, j: (0, j)),
            pl.BlockSpec((d, tf), lambda bi, i, j: (0, j + nf)),
            pl.BlockSpec((tf, d), lambda bi, i, j: (j, 0)),
        ],
        out_specs=pl.BlockSpec((1, tm, d), lambda bi, i, j: (bi, i, 0)),
        scratch_shapes=[pltpu.VMEM((tm, d), bf16), pltpu.VMEM((tm, d), f32)],
        compiler_params=_cparams(("parallel", "parallel", "arbitrary")),
        name=name,
    )(x, gain, sc, sh, g, w_in, w_in, w_out)


def _rope_tables(l):
    n_freq = DH_A // 4
    inv = ROPE_BASE ** (-jnp.arange(n_freq, dtype=f32) / n_freq)
    t = jnp.arange(l)
    row = (t // GRID_W).astype(f32)
    col = (t % GRID_W).astype(f32)
    ang = jnp.concatenate([row[:, None] * inv, col[:, None] * inv], axis=-1)
    cos = jnp.repeat(jnp.cos(ang), 2, axis=-1)
    sin = jnp.repeat(jnp.sin(ang), 2, axis=-1)
    sign = jnp.tile(jnp.array([-1.0, 1.0], f32), DH_A // 2)
    return jnp.tile(cos, (1, 2)), jnp.tile(sin * sign, (1, 2))


def _pad_rows(a, rows):
    return jnp.zeros((rows,) + a.shape[1:], a.dtype).at[:a.shape[0]].set(a)


def kernel(x, c, ctx, c_ctx, ada_w, ada_b, norm_mix, norm_ffn, ffn_w_in, ffn_w_out, even_w_in, even_w_out,
           diff_qk_gain, diff_lambda, diff_subln, gdn_conv, gdn_a_log, gdn_dt_bias, gdn_norm, odd_w_in,
           odd_w_out, na_qk_gain, na_rpb):
    b, l, d = x.shape
    lc = ctx.shape[1]
    depth = ada_w.shape[0]
    c_all = _pad_rows(jnp.concatenate([c, c_ctx[None]], axis=0), 16)
    mods = ada_modulation(c_all, ada_w, ada_b)
    cos_t, sin_t = _rope_tables(l)
    ctx_flat = ctx.reshape(1, b * lc, d)

    for layer in range(depth):
        ctx_out = layer < depth - 1
        m_lat = [t[:, None, :] for t in jnp.split(mods[layer, :b], 6, axis=-1)]
        m_ctx = [t[:, None, :] for t in jnp.split(mods[layer, b:b + 1], 6, axis=-1)]
        sh_m, sc_m, g_m, sh_f, sc_f, g_f = m_lat
        csh_m, csc_m, cg_m, csh_f, csc_f, cg_f = m_ctx
        gain_m = norm_mix[layer][None]
        gain_f = norm_ffn[layer][None]
        if layer % 2 == 0:
            e = layer // 2
            lam_init = 0.8 - 0.6 * math.exp(-0.3 * layer)
            w_in = even_w_in[e].astype(bf16)
            cuts = [0, A_W, 2 * A_W, 3 * A_W, 3 * A_W + 3 * B_W, 3 * A_W + 4 * B_W]
            ws = [w_in[:, cuts[i]:cuts[i + 1]] for i in range(5)]
            w_gates = jnp.zeros((d, LANES), bf16).at[:, :4 * H_B].set(w_in[:, cuts[5]:])
            ws.append(w_gates)
            qk_gain = _pad_rows(jnp.tile(diff_qk_gain[e], (1, 2)), 8)
            dts = [bf16, bf16, bf16, f32, f32, f32]
            q_scale = DH_A ** -0.5 * math.log2(math.e)
            epi = lambda rope: [("submap", 0, rope, q_scale), ("submap", 1, rope, 1.0), ("plain",),
                                ("plain",), ("plain",), ("plain",)]
            qa, ka, va, qkv_b, g_b, gates = norm_mod_project(
                x, gain_m, sc_m, sh_m, ws, dts, epi(True), qk_gain, rope_tabs=(cos_t, sin_t), name="even_proj")
            qac, kac, vac, qkv_bc, g_bc, gates_c = norm_mod_project(
                ctx_flat, gain_m, csc_m, csh_m, ws, dts, epi(False), qk_gain, name="even_proj_ctx")
            unflat = lambda t: t.reshape(b, lc, t.shape[-1])
            qac, kac, vac, qkv_bc, g_bc, gates_c = map(unflat, (qac, kac, vac, qkv_bc, g_bc, gates_c))

            lv = diff_lambda[e]
            lam = jnp.exp(jnp.sum(lv[0] * lv[1])) - jnp.exp(jnp.sum(lv[2] * lv[3])) + lam_init
            gmax = jnp.max(jnp.abs(diff_qk_gain[e]), axis=-1)
            bound = 1.02 * DH_A * q_scale * gmax[0] * gmax[1]
            scal = jnp.stack([lam, bound]).astype(f32)
            subln = diff_subln[e][None]
            attn = functools.partial(diff_attention, scal, qa, kac, vac, ka, va, subln, lam_init)
            a_lat = lax.cond(2.0 * bound < 120.0,
                             lambda: attn(bounded=True, tk=1024, name="diff_attn"),
                             lambda: attn(bounded=False, name="diff_attn_online"))
            a_ctx = diff_attention(scal, qac, kac, vac, None, None, subln, lam_init, name="diff_attn_ctx")

            y_lat = gdn_short_conv(qkv_b, gdn_conv[e])
            y_ctx = gdn_short_conv(qkv_bc, gdn_conv[e])
            per_pos = lambda p: jnp.repeat(p, GDN_CHUNK)
            prm = jnp.stack([per_pos(gdn_a_log[e][0]), per_pos(gdn_dt_bias[e][0]),
                             per_pos(gdn_a_log[e][1]), per_pos(gdn_dt_bias[e][1])])

            def to_rows(g):
                t = g[:, :, :4 * H_B].reshape(g.shape[0], -1, GDN_CHUNK, 4, H_B).transpose(0, 1, 3, 4, 2)
                t = t.reshape(g.shape[0], -1, 4, H_B * GDN_CHUNK)
                return t[:, :, jnp.array([0, 2, 1, 3])]

            s0 = jnp.zeros((b, 2, H_B * DH_B, DH_B), f32)
            ocf, ocb, s_mid = gdn_scan(y_ctx, to_rows(gates_c), prm, s0)
            olf, olb, _ = gdn_scan(y_lat, to_rows(gates), prm, s_mid)

            w_out = even_w_out[e].astype(bf16)
            gdn_gain = gdn_norm[e][None]
            x = out_project(x, g_m, a_lat, w_out[:A_W], gdn=(olf, olb, g_b, gdn_gain, w_out[A_W:]),
                            name="even_out")
            ctx_flat = out_project(
                ctx_flat, cg_m, a_ctx.reshape(1, b * lc, A_W), w_out[:A_W],
                gdn=(ocf.reshape(1, b * lc, B_W), ocb.reshape(1, b * lc, B_W), g_bc.reshape(1, b * lc, B_W),
                     gdn_gain, w_out[A_W:]), name="even_out_ctx")
        else:
            od = layer // 2
            w_in = odd_w_in[od].astype(bf16)
            mix = H_C * DH_C
            ws = [w_in[:, :mix], w_in[:, mix:2 * mix], w_in[:, 2 * mix:]]
            qk_gain = _pad_rows(na_qk_gain[od], 8)
            q, k, v = norm_mod_project(
                x, gain_m, sc_m, sh_m, ws, [bf16] * 3,
                [("headnorm", 0, DH_C ** -0.5), ("headnorm", 1, 1.0), ("plain",)], qk_gain, name="odd_proj")
            kc, vc = norm_mod_project(
                ctx_flat, gain_m, csc_m, csh_m, ws[1:], [bf16] * 2,
                [("headnorm", 1, 1.0), ("plain",)], qk_gain, name="odd_proj_ctx")
            kc, vc = kc.reshape(b, lc, mix), vc.reshape(b, lc, mix)
            o = neighbourhood_attention(q, k, v, kc, vc, na_rpb[od])
            x = out_project(x, g_m, o, odd_w_out[od].astype(bf16), name="odd_out")
            if ctx_out:
                raise NotImplementedError("context output of a neighbourhood layer is not needed at depth 2")

        wf_in = ffn_w_in[layer].astype(bf16)
        wf_out = ffn_w_out[layer].astype(bf16)
        x = ffn(x, gain_f, sc_f, sh_f, g_f, wf_in, wf_out, name="ffn")
        if ctx_out:
            ctx_flat = ffn(ctx_flat, gain_f, csc_f, csh_f, cg_f, wf_in, wf_out, name="ffn_ctx")
    return x
```

```python
import functools
import math

import numpy as np
import jax
import jax.numpy as jnp
from jax import lax
from jax.experimental import pallas as pl
from jax.experimental.pallas import tpu as pltpu

f32 = jnp.float32
bf16 = jnp.bfloat16

EPS = 1e-6
ROPE_BASE = 10000.0
GRID_W = 64
H_A = 4
DH_A = 64
DV_A = 2 * DH_A
A_W = H_A * DV_A
H_B = 4
DH_B = 128
B_W = H_B * DH_B
GDN_CHUNK = 64
CONV_K = 5
H_C = 8
DH_C = 128
WIN_R = 8
WIN_C = 16

LANES = 128
VMEM_LIMIT = 56 * 1024 * 1024
NEG = -1e30


def _cparams(sem):
    return pltpu.CompilerParams(dimension_semantics=sem, vmem_limit_bytes=VMEM_LIMIT)


def _silu(x):
    return x * (1.0 / (1.0 + jnp.exp(-x)))


def _dot(a, b):
    return jnp.dot(a, b, preferred_element_type=f32)


def _dot_nt(a, b):
    return lax.dot_general(a, b, (((1,), (1,)), ((), ())), preferred_element_type=f32)


def _split_bf16(x):
    hi = x.astype(bf16)
    lo = (x - hi.astype(f32)).astype(bf16)
    return hi, lo


def _dot3(a, b):
    ah, al = _split_bf16(a)
    bh, bl = _split_bf16(b)
    return _dot(ah, bh) + (_dot(ah, bl) + _dot(al, bh))


def _mm(a, b):
    return _dot(a.astype(bf16), b.astype(bf16))


def _dot3_nt(a, b):
    ah, al = _split_bf16(a)
    bh, bl = _split_bf16(b)
    return _dot_nt(ah, bh) + (_dot_nt(ah, bl) + _dot_nt(al, bh))


def _ada_kernel(c_ref, w_ref, b_ref, o_ref):
    sc = _silu(c_ref[...])
    o_ref[0] = _dot3(sc, w_ref[0]) + b_ref[0]


def ada_modulation(c_all, ada_w, ada_b):
    depth, d, n = ada_w.shape
    rows = c_all.shape[0]
    tn = 1536
    return pl.pallas_call(
        _ada_kernel,
        out_shape=jax.ShapeDtypeStruct((depth, rows, n), f32),
        grid=(depth, n // tn),
        in_specs=[
            pl.BlockSpec((rows, d), lambda l, j: (0, 0)),
            pl.BlockSpec((1, d, tn), lambda l, j: (l, 0, j)),
            pl.BlockSpec((1, 1, tn), lambda l, j: (l, 0, j)),
        ],
        out_specs=pl.BlockSpec((1, rows, tn), lambda l, j: (l, 0, j)),
        compiler_params=_cparams(("parallel", "parallel")),
        name="ada_modulation",
    )(c_all, ada_w, ada_b.reshape(depth, 1, n))


def _norm_mod(x, gain, sc, sh):
    y = x * lax.rsqrt(jnp.mean(x * x, axis=-1, keepdims=True) + EPS)
    return (y * gain) * (1.0 + sc) + sh


def _swap_pairs(x):
    lane = lax.broadcasted_iota(jnp.int32, x.shape, x.ndim - 1)
    nxt = pltpu.roll(x, x.shape[-1] - 1, x.ndim - 1)
    prv = pltpu.roll(x, 1, x.ndim - 1)
    return jnp.where(lane % 2 == 0, nxt, prv)


def _submap_norm_rope(t, gain, cos, sin, scale):
    lane = lax.broadcasted_iota(jnp.int32, t.shape, 1)
    lo = lane < DH_A
    sq = t * t
    s_lo = jnp.sum(jnp.where(lo, sq, 0.0), axis=-1, keepdims=True)
    s_hi = jnp.sum(jnp.where(lo, 0.0, sq), axis=-1, keepdims=True)
    r = jnp.where(lo, lax.rsqrt(s_lo * (1.0 / DH_A) + EPS), lax.rsqrt(s_hi * (1.0 / DH_A) + EPS))
    y = t * r * gain
    if cos is not None:
        y = y * cos + _swap_pairs(y) * sin
    if scale != 1.0:
        y = y * scale
    return y


def _head_norm(t, gain, scale):
    y = t * lax.rsqrt(jnp.mean(t * t, axis=-1, keepdims=True) + EPS) * gain
    if scale != 1.0:
        y = y * scale
    return y


def _proj_kernel(*refs, epilogues, rope):
    n_out = len(epilogues)
    x_ref, gain_ref, sc_ref, sh_ref = refs[:4]
    pos = 4
    if rope:
        cos_ref, sin_ref = refs[pos:pos + 2]
        pos += 2
    qkg_ref = refs[pos]
    pos += 1
    w_refs = refs[pos:pos + n_out]
    o_refs = refs[pos + n_out:pos + 2 * n_out]

    h = _norm_mod(x_ref[0], gain_ref[...], sc_ref[0], sh_ref[0]).astype(bf16)
    for w_ref, o_ref, epi in zip(w_refs, o_refs, epilogues):
        acc = _dot(h, w_ref[...])
        kind = epi[0]
        if kind == "plain":
            o_ref[0] = acc.astype(o_ref.dtype)
        elif kind == "submap":
            _, grow, use_rope, scale = epi
            gain = qkg_ref[grow:grow + 1, :]
            for hd in range(acc.shape[1] // LANES):
                t = acc[:, hd * LANES:(hd + 1) * LANES]
                cs = (cos_ref[...], sin_ref[...]) if (rope and use_rope) else (None, None)
                y = _submap_norm_rope(t, gain, cs[0], cs[1], scale)
                o_ref[0, :, hd * LANES:(hd + 1) * LANES] = y.astype(o_ref.dtype)
        elif kind == "headnorm":
            _, grow, scale = epi
            gain = qkg_ref[grow:grow + 1, :]
            for hd in range(acc.shape[1] // LANES):
                t = acc[:, hd * LANES:(hd + 1) * LANES]
                o_ref[0, :, hd * LANES:(hd + 1) * LANES] = _head_norm(t, gain, scale).astype(o_ref.dtype)
        else:
            raise ValueError(kind)


def norm_mod_project(x, gain, sc, sh, weights, out_dtypes, epilogues, qk_gain, rope_tabs=None,
                     tm=512, name="proj"):
    b, l, d = x.shape
    tm = min(tm, l)
    assert l % tm == 0
    bm = sc.shape[0]
    mod_map = (lambda bi, i: (bi, 0, 0)) if bm == b else (lambda bi, i: (0, 0, 0))
    rope = rope_tabs is not None
    in_specs = [
        pl.BlockSpec((1, tm, d), lambda bi, i: (bi, i, 0)),
        pl.BlockSpec((1, d), lambda bi, i: (0, 0)),
        pl.BlockSpec((1, 1, d), mod_map),
        pl.BlockSpec((1, 1, d), mod_map),
    ]
    args = [x, gain, sc, sh]
    if rope:
        in_specs += [pl.BlockSpec((tm, LANES), lambda bi, i: (i, 0))] * 2
        args += list(rope_tabs)
    in_specs.append(pl.BlockSpec(qk_gain.shape, lambda bi, i: (0, 0)))
    args.append(qk_gain)
    for w in weights:
        in_specs.append(pl.BlockSpec(w.shape, lambda bi, i: (0, 0)))
        args.append(w)
    out_shape = [jax.ShapeDtypeStruct((b, l, w.shape[1]), dt) for w, dt in zip(weights, out_dtypes)]
    out_specs = [pl.BlockSpec((1, tm, w.shape[1]), lambda bi, i: (bi, i, 0)) for w in weights]
    return pl.pallas_call(
        functools.partial(_proj_kernel, epilogues=tuple(epilogues), rope=rope),
        out_shape=out_shape,
        grid=(b, l // tm),
        in_specs=in_specs,
        out_specs=out_specs,
        compiler_params=_cparams(("parallel", "parallel")),
        name=name,
    )(*args)


def _diff_attn_kernel(*refs, has_lat, tk, lam_init, bounded):
    if has_lat:
        sc_ref, q_ref, kc_ref, vc_ref, k_ref, v_ref, subln_ref, o_ref = refs
    else:
        sc_ref, q_ref, kc_ref, vc_ref, subln_ref, o_ref = refs
    q = q_ref[0]
    tq = q.shape[0]
    lane = lax.broadcasted_iota(jnp.int32, q.shape, 1)
    zero = jnp.zeros_like(q)
    q1 = jnp.where(lane < DH_A, q, zero)
    q2 = jnp.where(lane < DH_A, zero, q)

    def kv_loop(step, carry):
        carry = step(kc_ref[0], vc_ref[0], carry)
        if has_lat:
            def body(j, c):
                off = pl.multiple_of(j * tk, tk)
                return step(k_ref[0, pl.ds(off, tk), :], v_ref[0, pl.ds(off, tk), :], c)

            carry = lax.fori_loop(0, k_ref.shape[1] // tk, body, carry)
        return carry

    if bounded:
        bound = sc_ref[1]

        q12 = jnp.concatenate([q1, q2], axis=0)

        def step(kblk, vblk, acc):
            ones_col = jnp.where(lax.broadcasted_iota(jnp.int32, vblk.shape, 1) == 0, 1.0, 0.0).astype(bf16)
            v_aug = jnp.concatenate([vblk, ones_col], axis=1)
            return acc + _dot(jnp.exp2(_dot_nt(q12, kblk) - bound).astype(bf16), v_aug)

        acc = kv_loop(step, jnp.zeros((2 * tq, 2 * DV_A), f32))
        den = jnp.sum(acc[:, DV_A:], axis=-1, keepdims=True)
        l1, l2 = den[:tq], den[tq:]
        a1, a2 = acc[:tq, :DV_A], acc[tq:, :DV_A]
    else:
        def one_map(qm, kblk, vblk, m, l, acc):
            s = _dot_nt(qm, kblk)
            m_new = jnp.maximum(m, jnp.max(s, axis=-1, keepdims=True))
            p = jnp.exp2(s - m_new)
            a = jnp.exp2(m - m_new)
            l = a * l + jnp.sum(p, axis=-1, keepdims=True)
            acc = a * acc + _dot(p.astype(bf16), vblk)
            return m_new, l, acc

        def step(kblk, vblk, carry):
            m1, l1, a1, m2, l2, a2 = carry
            m1, l1, a1 = one_map(q1, kblk, vblk, m1, l1, a1)
            m2, l2, a2 = one_map(q2, kblk, vblk, m2, l2, a2)
            return m1, l1, a1, m2, l2, a2

        mi = jnp.full((tq, 1), NEG, f32)
        li = jnp.zeros((tq, 1), f32)
        ai = jnp.zeros((tq, DV_A), f32)
        _, l1, a1, _, l2, a2 = kv_loop(step, (mi, li, ai, mi, li, ai))
    lam = sc_ref[0]
    o = a1 * (1.0 / l1) - lam * (a2 * (1.0 / l2))
    y = o * lax.rsqrt(jnp.mean(o * o, axis=-1, keepdims=True) + EPS) * subln_ref[...]
    o_ref[0] = (y * (1.0 - lam_init)).astype(o_ref.dtype)


def diff_attention(lam, q, k_ctx, v_ctx, k_lat, v_lat, subln, lam_init, bounded=False, tq=512, tk=512,
                   name="diff_attn"):
    b, lq, _ = q.shape
    lc = k_ctx.shape[1]
    has_lat = k_lat is not None
    tq = min(tq, lq)
    assert lq % tq == 0
    head_blk = lambda rows: pl.BlockSpec((1, rows, LANES), lambda bi, h, i: (bi, 0, h))
    in_specs = [
        pl.BlockSpec(memory_space=pltpu.SMEM),
        pl.BlockSpec((1, tq, LANES), lambda bi, h, i: (bi, i, h)),
        head_blk(lc), head_blk(lc),
    ]
    args = [lam, q, k_ctx, v_ctx]
    if has_lat:
        ll = k_lat.shape[1]
        tk = min(tk, ll)
        assert ll % tk == 0
        in_specs += [head_blk(ll), head_blk(ll)]
        args += [k_lat, v_lat]
    in_specs.append(pl.BlockSpec((1, LANES), lambda bi, h, i: (0, 0)))
    args.append(subln)
    return pl.pallas_call(
        functools.partial(_diff_attn_kernel, has_lat=has_lat, tk=tk, lam_init=lam_init, bounded=bounded),
        out_shape=jax.ShapeDtypeStruct((b, lq, A_W), bf16),
        grid=(b, H_A, lq // tq),
        in_specs=in_specs,
        out_specs=pl.BlockSpec((1, tq, LANES), lambda bi, h, i: (bi, i, h)),
        compiler_params=_cparams(("parallel", "parallel", "arbitrary")),
        name=name,
    )(*args)


def _gdn_conv_kernel(x_ref, w_ref, o_ref, pad_ref, *, rows_per_step):
    l = x_ref.shape[1]
    half = CONV_K // 2
    halo = 8
    zeros = jnp.zeros((halo, LANES), f32)
    pad_ref[0:halo, :] = zeros
    pad_ref[halo + l:halo + l + halo, :] = zeros
    pad_ref[halo:halo + l, :] = x_ref[0]
    kind = pl.program_id(1) // H_B
    w = w_ref[...]
    r = rows_per_step

    def body(i, _):
        t0 = pl.multiple_of(i * r, r)
        y = jnp.zeros((r, LANES), f32)
        for j in range(CONV_K):
            y = y + pad_ref[pl.ds(t0 + (halo - half + j), r), :] * w[j:j + 1, :]
        y = _silu(y)
        nrm = lax.rsqrt(jnp.sum(y * y, axis=-1, keepdims=True) + EPS)
        nrm = jnp.where(kind == 0, nrm * (DH_B ** -0.5), nrm)
        y = jnp.where(kind == 2, y, y * nrm)
        o_ref[0, pl.ds(t0, r), :] = y
        return 0

    lax.fori_loop(0, l // r, body, 0)


def gdn_short_conv(qkv, conv_w):
    b, l, c = qkv.shape
    r = min(512, l)
    assert l % r == 0
    wpad = jnp.zeros((8, c), f32).at[:CONV_K].set(conv_w)
    return pl.pallas_call(
        functools.partial(_gdn_conv_kernel, rows_per_step=r),
        out_shape=jax.ShapeDtypeStruct((b, l, c), f32),
        grid=(b, c // LANES),
        in_specs=[
            pl.BlockSpec((1, l, LANES), lambda bi, j: (bi, 0, j)),
            pl.BlockSpec((8, LANES), lambda bi, j: (0, j)),
        ],
        out_specs=pl.BlockSpec((1, l, LANES), lambda bi, j: (bi, 0, j)),
        scratch_shapes=[pltpu.VMEM((l + 16, LANES), f32)],
        compiler_params=_cparams(("parallel", "parallel")),
        name="gdn_conv",
    )(qkv, wpad)


def _softplus(x):
    return jnp.maximum(x, 0.0) + jnp.log1p(jnp.exp(-jnp.abs(x)))


def _sigmoid(x):
    return 1.0 / (1.0 + jnp.exp(-x))


def _stack_heads(x, base):
    return jnp.concatenate([x[:, base + h * DH_B:base + (h + 1) * DH_B] for h in range(H_B)], axis=0)


def _bmm(a, b):
    return jnp.einsum('bij,bjk->bik', a.astype(bf16), b.astype(bf16), preferred_element_type=f32)


def _bmm_nt(a, b):
    return jnp.einsum('bik,bjk->bij', a.astype(bf16), b.astype(bf16), preferred_element_type=f32)


def _gdn_chunk_prep(x, graw, prm, n_fwd):
    c = GDN_CHUNK
    n = H_B * c
    nb = x.shape[0]
    stack = lambda base: jnp.concatenate(
        [x[:, :, base + h * DH_B:base + (h + 1) * DH_B] for h in range(H_B)], axis=1)
    q, k, v = stack(0), stack(B_W), stack(2 * B_W)
    shp = (nb, n, n)
    chain = lax.broadcasted_iota(jnp.int32, shp, 0)
    r = lax.broadcasted_iota(jnp.int32, shp, 1)
    cc = lax.broadcasted_iota(jnp.int32, shp, 2)
    same = (r // c) == (cc // c)
    eye = r == cc
    ahead = jnp.where(chain < n_fwd, r - cc, cc - r)
    far = jnp.int32(4 * n)
    strict = jnp.where(same, ahead, -far) > 0
    incl = jnp.where(same, ahead, -far) >= 0
    incl_t = jnp.where(same, ahead, far) <= 0

    beta_r = _sigmoid(graw[:, 0:1, :])
    la_r = -jnp.exp(prm[:, 0:1, :]) * _softplus(graw[:, 1:2, :] + prm[:, 1:2, :])
    to_col = lambda row: jnp.sum(jnp.where(eye, row, 0.0), axis=2, keepdims=True)
    beta_c = to_col(beta_r)
    la_c = to_col(la_r)
    g_c = jnp.sum(jnp.where(incl, la_r, 0.0), axis=2, keepdims=True)
    g_r = jnp.sum(jnp.where(incl_t, la_c, 0.0), axis=1, keepdims=True)
    gtot_c = jnp.sum(jnp.where(same, la_r, 0.0), axis=2, keepdims=True)
    decay = jnp.where(incl, jnp.exp(jnp.where(incl, g_c - g_r, 0.0)), 0.0)

    kb = k.astype(bf16)
    kq_k = _bmm_nt(jnp.concatenate([kb, q.astype(bf16)], axis=1), kb)
    lmat = jnp.where(strict, beta_c * decay * kq_k[:, :n], 0.0)
    eg = jnp.exp(g_c)
    rhs = jnp.concatenate([beta_c * v, (beta_c * eg) * k], axis=-1)
    half = jnp.where((r // 2) == (cc // 2), lmat, 0.0)
    t = jnp.where(eye, 1.0, 0.0) - half
    kb_ = 2
    while kb_ < c:
        ck = jnp.where(((r // (2 * kb_)) == (cc // (2 * kb_))) & ((r // kb_) != (cc // kb_)), lmat, 0.0)
        t = t - _bmm(_bmm(t, ck), t)
        kb_ *= 2
    sol = _bmm(t, rhs)
    u0, w = sol[:, :, :DH_B], sol[:, :, DH_B:]

    aqk = (kq_k[:, n:] * decay).astype(bf16)
    qg = q * eg
    kg = k * jnp.exp(gtot_c - g_c)
    wide = (nb, n, H_B * DH_B)
    blk = (lax.broadcasted_iota(jnp.int32, wide, 1) // c) == (lax.broadcasted_iota(jnp.int32, wide, 2) // DH_B)
    spread = lambda m: jnp.where(blk, jnp.concatenate([m] * H_B, axis=2), 0.0).astype(bf16)
    tall = (nb, H_B * DH_B, n)
    blk_t = (lax.broadcasted_iota(jnp.int32, tall, 1) // DH_B) == (lax.broadcasted_iota(jnp.int32, tall, 2) // c)
    kg_t = jnp.stack([kg[i].T for i in range(nb)])
    kgt = jnp.where(blk_t, jnp.concatenate([kg_t] * H_B, axis=1), 0.0).astype(bf16)
    gl = jnp.exp(jnp.sum(jnp.where(blk_t, la_r, 0.0), axis=2, keepdims=True))
    return u0, jnp.concatenate([spread(w), spread(qg)], axis=1), jnp.concatenate([aqk, kgt], axis=1), gl


def _gdn_chunk_step(prep, idx, s_st):
    pick = lambda t: jnp.stack([t[i] for i in idx])
    u0, wq_bd, ak, gl = (pick(t) for t in prep)
    n = H_B * GDN_CHUNK
    ws_qs = _bmm(wq_bd, s_st)
    ub = (u0 - ws_qs[:, :n]).astype(bf16)
    au_ku = _bmm(ak, ub)
    o = ws_qs[:, n:] + au_ku[:, :n]
    return o, gl * s_st + au_ku[:, n:]


def _gdn_kernel(xf_ref, xb_ref, grf_ref, grb_ref, prm_ref, s0_ref, of_ref, ob_ref, s_ref, *, chunks):
    @pl.when(pl.program_id(1) == 0)
    def _():
        s_ref[...] = s0_ref[...]

    c = GDN_CHUNK
    rows = lambda j: slice(j * c, (j + 1) * c)
    x = jnp.stack([xf_ref[0, rows(j)] for j in range(chunks)] + [xb_ref[0, rows(j)] for j in range(chunks)])
    graw = jnp.stack([grf_ref[0, j, 0:2] for j in range(chunks)] + [grb_ref[0, j, 2:4] for j in range(chunks)])
    prm = jnp.stack([prm_ref[0:2]] * chunks + [prm_ref[2:4]] * chunks)
    prep = _gdn_chunk_prep(x, graw, prm, chunks)
    s = s_ref[0]
    heads_to_lanes = lambda o: jnp.concatenate([o[h * c:(h + 1) * c] for h in range(H_B)], axis=1)
    for j in range(chunks):
        jb = chunks - 1 - j
        o, s = _gdn_chunk_step(prep, (j, chunks + jb), s)
        of_ref[0, rows(j)] = heads_to_lanes(o[0])
        ob_ref[0, rows(jb)] = heads_to_lanes(o[1])
    s_ref[0] = s


def gdn_scan(x, gates_row, prm, s0, chunks=2):
    b, l, _ = x.shape
    c = GDN_CHUNK
    assert l % (c * chunks) == 0
    nb = l // (c * chunks)
    rows = c * chunks
    fwd = lambda bi, i: (bi, i, 0)
    bwd = lambda bi, i: (bi, nb - 1 - i, 0)
    st_spec = pl.BlockSpec((1, 2, H_B * DH_B, DH_B), lambda bi, i: (bi, 0, 0, 0))
    g_blk = (1, chunks, 4, H_B * c)
    return pl.pallas_call(
        functools.partial(_gdn_kernel, chunks=chunks),
        out_shape=[jax.ShapeDtypeStruct((b, l, B_W), f32), jax.ShapeDtypeStruct((b, l, B_W), f32),
                   jax.ShapeDtypeStruct(s0.shape, f32)],
        grid=(b, nb),
        in_specs=[
            pl.BlockSpec((1, rows, 3 * B_W), fwd), pl.BlockSpec((1, rows, 3 * B_W), bwd),
            pl.BlockSpec(g_blk, lambda bi, i: (bi, i, 0, 0)),
            pl.BlockSpec(g_blk, lambda bi, i: (bi, nb - 1 - i, 0, 0)),
            pl.BlockSpec((4, H_B * c), lambda bi, i: (0, 0)),
            st_spec,
        ],
        out_specs=[pl.BlockSpec((1, rows, B_W), fwd), pl.BlockSpec((1, rows, B_W), bwd), st_spec],
        compiler_params=_cparams(("parallel", "arbitrary")),
        name="gdn_scan",
    )(x, x, gates_row, gates_row, prm, s0)


def _na_kernel(sc_ref, q_ref, k_ref, v_ref, kc_ref, vc_ref, bias_ref, o_ref, *, rb, band, rows, hp, bounded):
    i = pl.program_id(2)
    ub = jnp.clip(i * rb - WIN_R // 2, 0, rows - band)
    off = pl.multiple_of(ub * GRID_W, GRID_W)
    heads = lambda t: jnp.stack([t[:, h * DH_C:(h + 1) * DH_C] for h in range(hp)])
    q = heads(q_ref[0])
    kb = heads(k_ref[0, pl.ds(off, band * GRID_W), :])
    vb = heads(v_ref[0, pl.ds(off, band * GRID_W), :])
    kc = heads(kc_ref[0])
    vc = heads(vc_ref[0])
    bqk = lambda a, b_: jnp.einsum('hqd,hkd->hqk', a, b_, preferred_element_type=f32)
    bpv = lambda a, b_: jnp.einsum('hqk,hkd->hqd', a, b_, preferred_element_type=f32)
    s_lat = bqk(q, kb) + bias_ref[0]
    s_ctx = bqk(q, kc)
    if bounded:
        def with_ones(t):
            ones_col = jnp.where(lax.broadcasted_iota(jnp.int32, t.shape, 2) == 0, 1.0, 0.0).astype(bf16)
            return jnp.concatenate([t, ones_col], axis=2)

        bound = sc_ref[0]
        acc = (bpv(jnp.exp2(s_lat - bound).astype(bf16), with_ones(vb))
               + bpv(jnp.exp2(s_ctx - bound).astype(bf16), with_ones(vc)))
        o = acc[:, :, :DH_C] * (1.0 / jnp.sum(acc[:, :, DH_C:], axis=-1, keepdims=True))
    else:
        m = jnp.maximum(jnp.max(s_lat, axis=-1, keepdims=True), jnp.max(s_ctx, axis=-1, keepdims=True))
        p_lat = jnp.exp2(s_lat - m)
        p_ctx = jnp.exp2(s_ctx - m)
        den = jnp.sum(p_lat, axis=-1, keepdims=True) + jnp.sum(p_ctx, axis=-1, keepdims=True)
        o = (bpv(p_lat.astype(bf16), vb) + bpv(p_ctx.astype(bf16), vc)) * (1.0 / den)
    o_ref[0] = jnp.concatenate([o[h] for h in range(hp)], axis=-1).astype(o_ref.dtype)


def _na_bias_index(rows, rb, band):
    wr = min(WIN_R, rows)
    nblk = rows // rb
    cols = np.arange(GRID_W)
    c_start = np.clip(cols - WIN_C // 2, 0, GRID_W - WIN_C)
    col_ok = (cols[None, :] >= c_start[:, None]) & (cols[None, :] < c_start[:, None] + WIN_C)
    dc_idx = np.clip(cols[None, :] - cols[:, None] + WIN_C - 1, 0, 2 * WIN_C - 2)

    def geometry(i):
        ub = int(np.clip(i * rb - WIN_R // 2, 0, rows - band))
        qr = i * rb + np.arange(rb)
        kr = ub + np.arange(band)
        r_start = np.clip(qr - wr // 2, 0, rows - wr)
        row_ok = (kr[None, :] >= r_start[:, None]) & (kr[None, :] < r_start[:, None] + wr)
        dr = np.clip(kr[None, :] - qr[:, None] + WIN_R - 1, 0, 2 * WIN_R - 2)
        ok = row_ok[:, None, :, None] & col_ok[None, :, None, :]
        return ok.reshape(rb * GRID_W, band * GRID_W), dr

    reps = [0, min(1, nblk - 1), nblk - 1]
    geo = [geometry(i) for i in reps]
    for i in range(1, nblk - 1):
        g = geometry(i)
        assert all(np.array_equal(a, b_) for a, b_ in zip(g, geo[1]))
    ok = np.stack([g[0] for g in geo])
    dr = np.stack([g[1] for g in geo])
    dr_onehot = (dr[..., None] == np.arange(2 * WIN_R - 1)).astype(np.float32)
    dc_onehot = (dc_idx[None] == np.arange(2 * WIN_C - 1)[:, None, None]).astype(np.float32)
    return ok, dr_onehot, dc_onehot


def _na_bias_table(rpb, rows, rb, band):
    ok, dr_onehot, dc_onehot = _na_bias_index(rows, rb, band)
    hi = lax.Precision.HIGHEST
    by_col = jnp.einsum('hrc,cqk->hrqk', rpb.astype(f32), dc_onehot, precision=hi)
    slabs = jnp.einsum('vabr,hrqk->vhaqbk', dr_onehot, by_col, precision=hi)
    h = rpb.shape[0]
    return jnp.where(ok[:, None], slabs.reshape(3, h, rb * GRID_W, band * GRID_W) * math.log2(math.e), NEG)


def neighbourhood_attention(q, k, v, k_ctx, v_ctx, rpb, bound, bounded, rb=4):
    b, l, _ = q.shape
    lc = k_ctx.shape[1]
    rows = l // GRID_W
    band = rb + WIN_R - 1
    assert rows % rb == 0 and rows >= band and WIN_R <= rows
    nblk = rows // rb
    bias = _na_bias_table(rpb, rows, rb, band)
    tq = rb * GRID_W
    variant = lambda i: jnp.where(i == 0, 0, jnp.where(i == nblk - 1, 2, 1))
    hp = 4
    assert H_C % hp == 0
    hw = hp * DH_C
    head_blk = lambda n: pl.BlockSpec((1, n, hw), lambda bi, h, i: (bi, 0, h))
    return pl.pallas_call(
        functools.partial(_na_kernel, rb=rb, band=band, rows=rows, hp=hp, bounded=bounded),
        out_shape=jax.ShapeDtypeStruct((b, l, H_C * DH_C), bf16),
        grid=(b, H_C // hp, nblk),
        in_specs=[
            pl.BlockSpec(memory_space=pltpu.SMEM),
            pl.BlockSpec((1, tq, hw), lambda bi, h, i: (bi, i, h)),
            head_blk(l), head_blk(l), head_blk(lc), head_blk(lc),
            pl.BlockSpec((1, hp, tq, band * GRID_W), lambda bi, h, i: (variant(i), h, 0, 0)),
        ],
        out_specs=pl.BlockSpec((1, tq, hw), lambda bi, h, i: (bi, i, h)),
        compiler_params=_cparams(("parallel", "parallel", "arbitrary")),
        name="na_attn" if bounded else "na_attn_online",
    )(bound, q, k, v, k_ctx, v_ctx, bias)


def _outproj_kernel(*refs, has_gdn):
    if has_gdn:
        x_ref, g_ref, a_ref, wa_ref, of_ref, ob_ref, gate_ref, gain_ref, wb_ref, o_ref = refs
    else:
        x_ref, g_ref, a_ref, wa_ref, o_ref = refs
    acc = _dot(a_ref[0], wa_ref[...])
    if has_gdn:
        o = of_ref[0] + ob_ref[0]
        gate = gate_ref[0]
        parts = []
        for h in range(H_B):
            sl = slice(h * DH_B, (h + 1) * DH_B)
            t = o[:, sl]
            y = t * lax.rsqrt(jnp.mean(t * t, axis=-1, keepdims=True) + EPS) * gain_ref[...]
            parts.append((y * _silu(gate[:, sl])).astype(bf16))
        acc = acc + _dot(jnp.concatenate(parts, axis=-1), wb_ref[...])
    o_ref[0] = x_ref[0] + g_ref[0] * acc


def out_project(x, g, a, wa, gdn=None, tm=512, name="outproj"):
    b, l, d = x.shape
    tm = min(tm, l)
    assert l % tm == 0
    bm = g.shape[0]
    mod_map = (lambda bi, i: (bi, 0, 0)) if bm == b else (lambda bi, i: (0, 0, 0))
    row = lambda n: pl.BlockSpec((1, tm, n), lambda bi, i: (bi, i, 0))
    full = lambda arr: pl.BlockSpec(arr.shape, lambda bi, i: (0, 0))
    in_specs = [row(d), pl.BlockSpec((1, 1, d), mod_map), row(a.shape[2]), full(wa)]
    args = [x, g, a, wa]
    if gdn is not None:
        o_f, o_b, gate, gain, wb = gdn
        in_specs += [row(B_W), row(B_W), row(B_W), full(gain), full(wb)]
        args += [o_f, o_b, gate, gain, wb]
    return pl.pallas_call(
        functools.partial(_outproj_kernel, has_gdn=gdn is not None),
        out_shape=jax.ShapeDtypeStruct((b, l, d), f32),
        grid=(b, l // tm),
        in_specs=in_specs,
        out_specs=row(d),
        compiler_params=_cparams(("parallel", "parallel")),
        name=name,
    )(*args)


def _ffn_kernel(x_ref, gain_ref, sc_ref, sh_ref, g_ref, wg_ref, wu_ref, wo_ref, o_ref, h_ref, acc_ref):
    j = pl.program_id(2)

    @pl.when(j == 0)
    def _():
        h_ref[...] = _norm_mod(x_ref[0], gain_ref[...], sc_ref[0], sh_ref[0]).astype(bf16)
        acc_ref[...] = jnp.zeros_like(acc_ref)

    tm = h_ref.shape[0]
    halves = [slice(r * (tm // 2), (r + 1) * (tm // 2)) for r in range(2)] if tm >= 512 else [slice(0, tm)]
    gate_up = []
    for rows in halves:
        h = h_ref[rows, :]
        gate_up.append((_dot(h, wg_ref[...]), _dot(h, wu_ref[...])))
    for rows, (gt, up) in zip(halves, gate_up):
        acc_ref[rows, :] += _dot((_silu(gt) * up).astype(bf16), wo_ref[...])

    @pl.when(j == pl.num_programs(2) - 1)
    def _():
        o_ref[0] = x_ref[0] + g_ref[0] * acc_ref[...]


def ffn(x, gain, sc, sh, g, w_in, w_out, tm=1024, tf=256, name="ffn"):
    b, l, d = x.shape
    ff = w_out.shape[0]
    tm = min(tm, l)
    assert l % tm == 0 and ff % tf == 0
    nf = ff // tf
    bm = sc.shape[0]
    mod_map = (lambda bi, i, j: (bi, 0, 0)) if bm == b else (lambda bi, i, j: (0, 0, 0))
    mod = pl.BlockSpec((1, 1, d), mod_map)
    return pl.pallas_call(
        _ffn_kernel,
        out_shape=jax.ShapeDtypeStruct((b, l, d), f32),
        grid=(b, l // tm, nf),
        in_specs=[
            pl.BlockSpec((1, tm, d), lambda bi, i, j: (bi, i, 0)),
            pl.BlockSpec((1, d), lambda bi, i, j: (0, 0)),
            mod, mod, mod,
            pl.BlockSpec((d, tf), lambda bi, i, j: (0, j)),
            pl.BlockSpec((d, tf), lambda bi, i, j: (0, j + nf)),
            pl.BlockSpec((tf, d), lambda bi, i, j: (j, 0)),
        ],
        out_specs=pl.BlockSpec((1, tm, d), lambda bi, i, j: (bi, i, 0)),
        scratch_shapes=[pltpu.VMEM((tm, d), bf16), pltpu.VMEM((tm, d), f32)],
        compiler_params=_cparams(("parallel", "parallel", "arbitrary")),
        name=name,
    )(x, gain, sc, sh, g, w_in, w_in, w_out)


def _rope_tables(l):
    n_freq = DH_A // 4
    inv = ROPE_BASE ** (-jnp.arange(n_freq, dtype=f32) / n_freq)
    t = jnp.arange(l)
    row = (t // GRID_W).astype(f32)
    col = (t % GRID_W).astype(f32)
    ang = jnp.concatenate([row[:, None] * inv, col[:, None] * inv], axis=-1)
    cos = jnp.repeat(jnp.cos(ang), 2, axis=-1)
    sin = jnp.repeat(jnp.sin(ang), 2, axis=-1)
    sign = jnp.tile(jnp.array([-1.0, 1.0], f32), DH_A // 2)
    return jnp.tile(cos, (1, 2)), jnp.tile(sin * sign, (1, 2))


def _pad_rows(a, rows):
    return jnp.zeros((rows,) + a.shape[1:], a.dtype).at[:a.shape[0]].set(a)


def kernel(x, c, ctx, c_ctx, ada_w, ada_b, norm_mix, norm_ffn, ffn_w_in, ffn_w_out, even_w_in, even_w_out,
           diff_qk_gain, diff_lambda, diff_subln, gdn_conv, gdn_a_log, gdn_dt_bias, gdn_norm, odd_w_in,
           odd_w_out, na_qk_gain, na_rpb):
    b, l, d = x.shape
    lc = ctx.shape[1]
    depth = ada_w.shape[0]
    c_all = _pad_rows(jnp.concatenate([c, c_ctx[None]], axis=0), 16)
    mods = ada_modulation(c_all, ada_w, ada_b)
    cos_t, sin_t = _rope_tables(l)
    ctx_flat = ctx.reshape(1, b * lc, d)

    for layer in range(depth):
        ctx_out = layer < depth - 1
        m_lat = [t[:, None, :] for t in jnp.split(mods[layer, :b], 6, axis=-1)]
        m_ctx = [t[:, None, :] for t in jnp.split(mods[layer, b:b + 1], 6, axis=-1)]
        sh_m, sc_m, g_m, sh_f, sc_f, g_f = m_lat
        csh_m, csc_m, cg_m, csh_f, csc_f, cg_f = m_ctx
        gain_m = norm_mix[layer][None]
        gain_f = norm_ffn[layer][None]
        if layer % 2 == 0:
            e = layer // 2
            lam_init = 0.8 - 0.6 * math.exp(-0.3 * layer)
            w_in = even_w_in[e].astype(bf16)
            cuts = [0, A_W, 2 * A_W, 3 * A_W, 3 * A_W + 3 * B_W, 3 * A_W + 4 * B_W]
            ws = [w_in[:, cuts[i]:cuts[i + 1]] for i in range(5)]
            w_gates = jnp.zeros((d, LANES), bf16).at[:, :4 * H_B].set(w_in[:, cuts[5]:])
            ws.append(w_gates)
            qk_gain = _pad_rows(jnp.tile(diff_qk_gain[e], (1, 2)), 8)
            dts = [bf16, bf16, bf16, f32, f32, f32]
            q_scale = DH_A ** -0.5 * math.log2(math.e)
            epi = lambda rope: [("submap", 0, rope, q_scale), ("submap", 1, rope, 1.0), ("plain",),
                                ("plain",), ("plain",), ("plain",)]
            qa, ka, va, qkv_b, g_b, gates = norm_mod_project(
                x, gain_m, sc_m, sh_m, ws, dts, epi(True), qk_gain, rope_tabs=(cos_t, sin_t), name="even_proj")
            qac, kac, vac, qkv_bc, g_bc, gates_c = norm_mod_project(
                ctx_flat, gain_m, csc_m, csh_m, ws, dts, epi(False), qk_gain, name="even_proj_ctx")
            unflat = lambda t: t.reshape(b, lc, t.shape[-1])
            qac, kac, vac, qkv_bc, g_bc, gates_c = map(unflat, (qac, kac, vac, qkv_bc, g_bc, gates_c))

            lv = diff_lambda[e]
            lam = jnp.exp(jnp.sum(lv[0] * lv[1])) - jnp.exp(jnp.sum(lv[2] * lv[3])) + lam_init
            gmax = jnp.max(jnp.abs(diff_qk_gain[e]), axis=-1)
            bound = 1.02 * DH_A * q_scale * gmax[0] * gmax[1]
            scal = jnp.stack([lam, bound]).astype(f32)
            subln = diff_subln[e][None]
            attn = functools.partial(diff_attention, scal, qa, kac, vac, ka, va, subln, lam_init)
            a_lat = lax.cond(2.0 * bound < 120.0,
                             lambda: attn(bounded=True, tk=4096, name="diff_attn"),
                             lambda: attn(bounded=False, name="diff_attn_online"))
            a_ctx = diff_attention(scal, qac, kac, vac, None, None, subln, lam_init, name="diff_attn_ctx")

            y_lat = gdn_short_conv(qkv_b, gdn_conv[e])
            y_ctx = gdn_short_conv(qkv_bc, gdn_conv[e])
            per_pos = lambda p: jnp.repeat(p, GDN_CHUNK)
            prm = jnp.stack([per_pos(gdn_a_log[e][0]), per_pos(gdn_dt_bias[e][0]),
                             per_pos(gdn_a_log[e][1]), per_pos(gdn_dt_bias[e][1])])

            def to_rows(g):
                t = g[:, :, :4 * H_B].reshape(g.shape[0], -1, GDN_CHUNK, 4, H_B).transpose(0, 1, 3, 4, 2)
                t = t.reshape(g.shape[0], -1, 4, H_B * GDN_CHUNK)
                return t[:, :, jnp.array([0, 2, 1, 3])]

            s0 = jnp.zeros((b, 2, H_B * DH_B, DH_B), f32)
            ocf, ocb, s_mid = gdn_scan(y_ctx, to_rows(gates_c), prm, s0)
            olf, olb, _ = gdn_scan(y_lat, to_rows(gates), prm, s_mid)

            w_out = even_w_out[e].astype(bf16)
            gdn_gain = gdn_norm[e][None]
            x = out_project(x, g_m, a_lat, w_out[:A_W], gdn=(olf, olb, g_b, gdn_gain, w_out[A_W:]),
                            name="even_out")
            ctx_flat = out_project(
                ctx_flat, cg_m, a_ctx.reshape(1, b * lc, A_W), w_out[:A_W],
                gdn=(ocf.reshape(1, b * lc, B_W), ocb.reshape(1, b * lc, B_W), g_bc.reshape(1, b * lc, B_W),
                     gdn_gain, w_out[A_W:]), name="even_out_ctx")
        else:
            od = layer // 2
            w_in = odd_w_in[od].astype(bf16)
            mix = H_C * DH_C
            ws = [w_in[:, :mix], w_in[:, mix:2 * mix], w_in[:, 2 * mix:]]
            qk_gain = _pad_rows(na_qk_gain[od], 8)
            q_scale = DH_C ** -0.5 * math.log2(math.e)
            q, k, v = norm_mod_project(
                x, gain_m, sc_m, sh_m, ws, [bf16] * 3,
                [("headnorm", 0, q_scale), ("headnorm", 1, 1.0), ("plain",)], qk_gain, name="odd_proj")
            kc, vc = norm_mod_project(
                ctx_flat, gain_m, csc_m, csh_m, ws[1:], [bf16] * 2,
                [("headnorm", 1, 1.0), ("plain",)], qk_gain, name="odd_proj_ctx")
            kc, vc = kc.reshape(b, lc, mix), vc.reshape(b, lc, mix)
            gmax = jnp.max(jnp.abs(na_qk_gain[od]), axis=-1)
            bound = (1.02 * DH_C * q_scale * gmax[0] * gmax[1]
                     + jnp.max(jnp.abs(na_rpb[od])) * math.log2(math.e)).astype(f32).reshape(1)
            na = functools.partial(neighbourhood_attention, q, k, v, kc, vc, na_rpb[od], bound)
            o = lax.cond(2.0 * bound[0] < 120.0, lambda: na(True), lambda: na(False))
            x = out_project(x, g_m, o, odd_w_out[od].astype(bf16), name="odd_out")
            if ctx_out:
                raise NotImplementedError("context output of a neighbourhood layer is not needed at depth 2")

        wf_in = ffn_w_in[layer].astype(bf16)
        wf_out = ffn_w_out[layer].astype(bf16)
        x = ffn(x, gain_f, sc_f, sh_f, g_f, wf_in, wf_out, name="ffn")
        if ctx_out:
            ctx_flat = ffn(ctx_flat, gain_f, csc_f, csh_f, cg_f, wf_in, wf_out, name="ffn_ctx")
    return x
```

```python
import functools
import math

import numpy as np
import jax
import jax.numpy as jnp
from jax import lax
from jax.experimental import pallas as pl
from jax.experimental.pallas import tpu as pltpu

f32 = jnp.float32
bf16 = jnp.bfloat16

EPS = 1e-6
ROPE_BASE = 10000.0
GRID_W = 64
H_A = 4
DH_A = 64
DV_A = 2 * DH_A
A_W = H_A * DV_A
H_B = 4
DH_B = 128
B_W = H_B * DH_B
GDN_CHUNK = 64
CONV_K = 5
H_C = 8
DH_C = 128
WIN_R = 8
WIN_C = 16

LANES = 128
VMEM_LIMIT = 56 * 1024 * 1024
NEG = -1e30


def _cparams(sem):
    return pltpu.CompilerParams(dimension_semantics=sem, vmem_limit_bytes=VMEM_LIMIT)


def _silu(x):
    return x * (1.0 / (1.0 + jnp.exp(-x)))


def _dot(a, b):
    return jnp.dot(a, b, preferred_element_type=f32)


def _dot_nt(a, b):
    return lax.dot_general(a, b, (((1,), (1,)), ((), ())), preferred_element_type=f32)


def _split_bf16(x):
    hi = x.astype(bf16)
    lo = (x - hi.astype(f32)).astype(bf16)
    return hi, lo


def _dot3(a, b):
    ah, al = _split_bf16(a)
    bh, bl = _split_bf16(b)
    return _dot(ah, bh) + (_dot(ah, bl) + _dot(al, bh))


def _mm(a, b):
    return _dot(a.astype(bf16), b.astype(bf16))


def _dot3_nt(a, b):
    ah, al = _split_bf16(a)
    bh, bl = _split_bf16(b)
    return _dot_nt(ah, bh) + (_dot_nt(ah, bl) + _dot_nt(al, bh))


def _ada_kernel(c_ref, w_ref, b_ref, o_ref):
    sc = _silu(c_ref[...])
    o_ref[0] = _dot3(sc, w_ref[0]) + b_ref[0]


def ada_modulation(c_all, ada_w, ada_b):
    depth, d, n = ada_w.shape
    rows = c_all.shape[0]
    tn = 1536
    return pl.pallas_call(
        _ada_kernel,
        out_shape=jax.ShapeDtypeStruct((depth, rows, n), f32),
        grid=(depth, n // tn),
        in_specs=[
            pl.BlockSpec((rows, d), lambda l, j: (0, 0)),
            pl.BlockSpec((1, d, tn), lambda l, j: (l, 0, j)),
            pl.BlockSpec((1, 1, tn), lambda l, j: (l, 0, j)),
        ],
        out_specs=pl.BlockSpec((1, rows, tn), lambda l, j: (l, 0, j)),
        compiler_params=_cparams(("parallel", "parallel")),
        name="ada_modulation",
    )(c_all, ada_w, ada_b.reshape(depth, 1, n))


def _norm_mod(x, gain, sc, sh):
    y = x * lax.rsqrt(jnp.mean(x * x, axis=-1, keepdims=True) + EPS)
    return (y * gain) * (1.0 + sc) + sh


def _swap_pairs(x):
    lane = lax.broadcasted_iota(jnp.int32, x.shape, x.ndim - 1)
    nxt = pltpu.roll(x, x.shape[-1] - 1, x.ndim - 1)
    prv = pltpu.roll(x, 1, x.ndim - 1)
    return jnp.where(lane % 2 == 0, nxt, prv)


def _submap_norm_rope(t, gain, cos, sin, scale):
    lane = lax.broadcasted_iota(jnp.int32, t.shape, 1)
    lo = lane < DH_A
    sq = t * t
    s_lo = jnp.sum(jnp.where(lo, sq, 0.0), axis=-1, keepdims=True)
    s_hi = jnp.sum(jnp.where(lo, 0.0, sq), axis=-1, keepdims=True)
    r = jnp.where(lo, lax.rsqrt(s_lo * (1.0 / DH_A) + EPS), lax.rsqrt(s_hi * (1.0 / DH_A) + EPS))
    y = t * r * gain
    if cos is not None:
        y = y * cos + _swap_pairs(y) * sin
    if scale != 1.0:
        y = y * scale
    return y


def _head_norm(t, gain, scale):
    y = t * lax.rsqrt(jnp.mean(t * t, axis=-1, keepdims=True) + EPS) * gain
    if scale != 1.0:
        y = y * scale
    return y


def _proj_kernel(*refs, epilogues, rope):
    n_out = len(epilogues)
    x_ref, gain_ref, sc_ref, sh_ref = refs[:4]
    pos = 4
    if rope:
        cos_ref, sin_ref = refs[pos:pos + 2]
        pos += 2
    qkg_ref = refs[pos]
    pos += 1
    w_refs = refs[pos:pos + n_out]
    o_refs = refs[pos + n_out:pos + 2 * n_out]

    h = _norm_mod(x_ref[0], gain_ref[...], sc_ref[0], sh_ref[0]).astype(bf16)
    for w_ref, o_ref, epi in zip(w_refs, o_refs, epilogues):
        acc = _dot(h, w_ref[...])
        kind = epi[0]
        if kind == "plain":
            o_ref[0] = acc.astype(o_ref.dtype)
        elif kind == "submap":
            _, grow, use_rope, scale = epi
            gain = qkg_ref[grow:grow + 1, :]
            for hd in range(acc.shape[1] // LANES):
                t = acc[:, hd * LANES:(hd + 1) * LANES]
                cs = (cos_ref[...], sin_ref[...]) if (rope and use_rope) else (None, None)
                y = _submap_norm_rope(t, gain, cs[0], cs[1], scale)
                o_ref[0, :, hd * LANES:(hd + 1) * LANES] = y.astype(o_ref.dtype)
        elif kind == "headnorm":
            _, grow, scale = epi
            gain = qkg_ref[grow:grow + 1, :]
            for hd in range(acc.shape[1] // LANES):
                t = acc[:, hd * LANES:(hd + 1) * LANES]
                o_ref[0, :, hd * LANES:(hd + 1) * LANES] = _head_norm(t, gain, scale).astype(o_ref.dtype)
        else:
            raise ValueError(kind)


def norm_mod_project(x, gain, sc, sh, weights, out_dtypes, epilogues, qk_gain, rope_tabs=None,
                     tm=512, name="proj"):
    b, l, d = x.shape
    tm = min(tm, l)
    assert l % tm == 0
    bm = sc.shape[0]
    mod_map = (lambda bi, i: (bi, 0, 0)) if bm == b else (lambda bi, i: (0, 0, 0))
    rope = rope_tabs is not None
    in_specs = [
        pl.BlockSpec((1, tm, d), lambda bi, i: (bi, i, 0)),
        pl.BlockSpec((1, d), lambda bi, i: (0, 0)),
        pl.BlockSpec((1, 1, d), mod_map),
        pl.BlockSpec((1, 1, d), mod_map),
    ]
    args = [x, gain, sc, sh]
    if rope:
        in_specs += [pl.BlockSpec((tm, LANES), lambda bi, i: (i, 0))] * 2
        args += list(rope_tabs)
    in_specs.append(pl.BlockSpec(qk_gain.shape, lambda bi, i: (0, 0)))
    args.append(qk_gain)
    for w in weights:
        in_specs.append(pl.BlockSpec(w.shape, lambda bi, i: (0, 0)))
        args.append(w)
    out_shape = [jax.ShapeDtypeStruct((b, l, w.shape[1]), dt) for w, dt in zip(weights, out_dtypes)]
    out_specs = [pl.BlockSpec((1, tm, w.shape[1]), lambda bi, i: (bi, i, 0)) for w in weights]
    return pl.pallas_call(
        functools.partial(_proj_kernel, epilogues=tuple(epilogues), rope=rope),
        out_shape=out_shape,
        grid=(b, l // tm),
        in_specs=in_specs,
        out_specs=out_specs,
        compiler_params=_cparams(("parallel", "parallel")),
        name=name,
    )(*args)


def _diff_attn_kernel(*refs, has_lat, tk, lam_init, bounded):
    if has_lat:
        sc_ref, q_ref, kc_ref, vc_ref, k_ref, v_ref, subln_ref, o_ref = refs
    else:
        sc_ref, q_ref, kc_ref, vc_ref, subln_ref, o_ref = refs
    q = q_ref[0]
    tq = q.shape[0]
    lane = lax.broadcasted_iota(jnp.int32, q.shape, 1)
    zero = jnp.zeros_like(q)
    q1 = jnp.where(lane < DH_A, q, zero)
    q2 = jnp.where(lane < DH_A, zero, q)

    def kv_loop(step, carry):
        carry = step(kc_ref[0], vc_ref[0], carry)
        if has_lat:
            def body(j, c):
                off = pl.multiple_of(j * tk, tk)
                return step(k_ref[0, pl.ds(off, tk), :], v_ref[0, pl.ds(off, tk), :], c)

            carry = lax.fori_loop(0, k_ref.shape[1] // tk, body, carry)
        return carry

    if bounded:
        bound = sc_ref[1]

        q12 = jnp.concatenate([q1, q2], axis=0)

        def step(kblk, vblk, acc):
            ones_col = jnp.where(lax.broadcasted_iota(jnp.int32, vblk.shape, 1) == 0, 1.0, 0.0).astype(bf16)
            v_aug = jnp.concatenate([vblk, ones_col], axis=1)
            return acc + _dot(jnp.exp2(_dot_nt(q12, kblk) - bound).astype(bf16), v_aug)

        acc = kv_loop(step, jnp.zeros((2 * tq, 2 * DV_A), f32))
        den = jnp.sum(acc[:, DV_A:], axis=-1, keepdims=True)
        l1, l2 = den[:tq], den[tq:]
        a1, a2 = acc[:tq, :DV_A], acc[tq:, :DV_A]
    else:
        def one_map(qm, kblk, vblk, m, l, acc):
            s = _dot_nt(qm, kblk)
            m_new = jnp.maximum(m, jnp.max(s, axis=-1, keepdims=True))
            p = jnp.exp2(s - m_new)
            a = jnp.exp2(m - m_new)
            l = a * l + jnp.sum(p, axis=-1, keepdims=True)
            acc = a * acc + _dot(p.astype(bf16), vblk)
            return m_new, l, acc

        def step(kblk, vblk, carry):
            m1, l1, a1, m2, l2, a2 = carry
            m1, l1, a1 = one_map(q1, kblk, vblk, m1, l1, a1)
            m2, l2, a2 = one_map(q2, kblk, vblk, m2, l2, a2)
            return m1, l1, a1, m2, l2, a2

        mi = jnp.full((tq, 1), NEG, f32)
        li = jnp.zeros((tq, 1), f32)
        ai = jnp.zeros((tq, DV_A), f32)
        _, l1, a1, _, l2, a2 = kv_loop(step, (mi, li, ai, mi, li, ai))
    lam = sc_ref[0]
    o = a1 * (1.0 / l1) - lam * (a2 * (1.0 / l2))
    y = o * lax.rsqrt(jnp.mean(o * o, axis=-1, keepdims=True) + EPS) * subln_ref[...]
    o_ref[0] = (y * (1.0 - lam_init)).astype(o_ref.dtype)


def diff_attention(lam, q, k_ctx, v_ctx, k_lat, v_lat, subln, lam_init, bounded=False, tq=512, tk=512,
                   name="diff_attn"):
    b, lq, _ = q.shape
    lc = k_ctx.shape[1]
    has_lat = k_lat is not None
    tq = min(tq, lq)
    assert lq % tq == 0
    head_blk = lambda rows: pl.BlockSpec((1, rows, LANES), lambda bi, h, i: (bi, 0, h))
    in_specs = [
        pl.BlockSpec(memory_space=pltpu.SMEM),
        pl.BlockSpec((1, tq, LANES), lambda bi, h, i: (bi, i, h)),
        head_blk(lc), head_blk(lc),
    ]
    args = [lam, q, k_ctx, v_ctx]
    if has_lat:
        ll = k_lat.shape[1]
        tk = min(tk, ll)
        assert ll % tk == 0
        in_specs += [head_blk(ll), head_blk(ll)]
        args += [k_lat, v_lat]
    in_specs.append(pl.BlockSpec((1, LANES), lambda bi, h, i: (0, 0)))
    args.append(subln)
    return pl.pallas_call(
        functools.partial(_diff_attn_kernel, has_lat=has_lat, tk=tk, lam_init=lam_init, bounded=bounded),
        out_shape=jax.ShapeDtypeStruct((b, lq, A_W), bf16),
        grid=(b, H_A, lq // tq),
        in_specs=in_specs,
        out_specs=pl.BlockSpec((1, tq, LANES), lambda bi, h, i: (bi, i, h)),
        compiler_params=_cparams(("parallel", "parallel", "arbitrary")),
        name=name,
    )(*args)


def _gdn_conv_kernel(x_ref, w_ref, o_ref, pad_ref, *, rows_per_step):
    l = x_ref.shape[1]
    half = CONV_K // 2
    halo = 8
    zeros = jnp.zeros((halo, LANES), f32)
    pad_ref[0:halo, :] = zeros
    pad_ref[halo + l:halo + l + halo, :] = zeros
    pad_ref[halo:halo + l, :] = x_ref[0]
    kind = pl.program_id(1) // H_B
    w = w_ref[...]
    r = rows_per_step

    def body(i, _):
        t0 = pl.multiple_of(i * r, r)
        y = jnp.zeros((r, LANES), f32)
        for j in range(CONV_K):
            y = y + pad_ref[pl.ds(t0 + (halo - half + j), r), :] * w[j:j + 1, :]
        y = _silu(y)
        nrm = lax.rsqrt(jnp.sum(y * y, axis=-1, keepdims=True) + EPS)
        nrm = jnp.where(kind == 0, nrm * (DH_B ** -0.5), nrm)
        y = jnp.where(kind == 2, y, y * nrm)
        o_ref[0, pl.ds(t0, r), :] = y
        return 0

    lax.fori_loop(0, l // r, body, 0)


def gdn_short_conv(qkv, conv_w):
    b, l, c = qkv.shape
    r = min(512, l)
    assert l % r == 0
    wpad = jnp.zeros((8, c), f32).at[:CONV_K].set(conv_w)
    return pl.pallas_call(
        functools.partial(_gdn_conv_kernel, rows_per_step=r),
        out_shape=jax.ShapeDtypeStruct((b, l, c), f32),
        grid=(b, c // LANES),
        in_specs=[
            pl.BlockSpec((1, l, LANES), lambda bi, j: (bi, 0, j)),
            pl.BlockSpec((8, LANES), lambda bi, j: (0, j)),
        ],
        out_specs=pl.BlockSpec((1, l, LANES), lambda bi, j: (bi, 0, j)),
        scratch_shapes=[pltpu.VMEM((l + 16, LANES), f32)],
        compiler_params=_cparams(("parallel", "parallel")),
        name="gdn_conv",
    )(qkv, wpad)


def _softplus(x):
    return jnp.maximum(x, 0.0) + jnp.log1p(jnp.exp(-jnp.abs(x)))


def _sigmoid(x):
    return 1.0 / (1.0 + jnp.exp(-x))


def _stack_heads(x, base):
    return jnp.concatenate([x[:, base + h * DH_B:base + (h + 1) * DH_B] for h in range(H_B)], axis=0)


def _bmm(a, b):
    return jnp.einsum('bij,bjk->bik', a.astype(bf16), b.astype(bf16), preferred_element_type=f32)


def _bmm_nt(a, b):
    return jnp.einsum('bik,bjk->bij', a.astype(bf16), b.astype(bf16), preferred_element_type=f32)


def _gdn_chunk_prep(x, graw, prm, n_fwd):
    c = GDN_CHUNK
    n = H_B * c
    nb = x.shape[0]
    stack = lambda base: jnp.concatenate(
        [x[:, :, base + h * DH_B:base + (h + 1) * DH_B] for h in range(H_B)], axis=1)
    q, k, v = stack(0), stack(B_W), stack(2 * B_W)
    shp = (nb, n, n)
    chain = lax.broadcasted_iota(jnp.int32, shp, 0)
    r = lax.broadcasted_iota(jnp.int32, shp, 1)
    cc = lax.broadcasted_iota(jnp.int32, shp, 2)
    same = (r // c) == (cc // c)
    eye = r == cc
    ahead = jnp.where(chain < n_fwd, r - cc, cc - r)
    far = jnp.int32(4 * n)
    strict = jnp.where(same, ahead, -far) > 0
    incl = jnp.where(same, ahead, -far) >= 0
    incl_t = jnp.where(same, ahead, far) <= 0

    beta_r = _sigmoid(graw[:, 0:1, :])
    la_r = -jnp.exp(prm[:, 0:1, :]) * _softplus(graw[:, 1:2, :] + prm[:, 1:2, :])
    to_col = lambda row: jnp.sum(jnp.where(eye, row, 0.0), axis=2, keepdims=True)
    beta_c = to_col(beta_r)
    la_c = to_col(la_r)
    g_c = jnp.sum(jnp.where(incl, la_r, 0.0), axis=2, keepdims=True)
    g_r = jnp.sum(jnp.where(incl_t, la_c, 0.0), axis=1, keepdims=True)
    gtot_c = jnp.sum(jnp.where(same, la_r, 0.0), axis=2, keepdims=True)
    decay = jnp.where(incl, jnp.exp(jnp.where(incl, g_c - g_r, 0.0)), 0.0)

    kb = k.astype(bf16)
    kq_k = _bmm_nt(jnp.concatenate([kb, q.astype(bf16)], axis=1), kb)
    lmat = jnp.where(strict, beta_c * decay * kq_k[:, :n], 0.0)
    eg = jnp.exp(g_c)
    rhs = jnp.concatenate([beta_c * v, (beta_c * eg) * k], axis=-1)
    half = jnp.where((r // 2) == (cc // 2), lmat, 0.0)
    t = jnp.where(eye, 1.0, 0.0) - half
    kb_ = 2
    while kb_ < c:
        ck = jnp.where(((r // (2 * kb_)) == (cc // (2 * kb_))) & ((r // kb_) != (cc // kb_)), lmat, 0.0)
        t = t - _bmm(_bmm(t, ck), t)
        kb_ *= 2
    sol = _bmm(t, rhs)
    u0, w = sol[:, :, :DH_B], sol[:, :, DH_B:]

    aqk = (kq_k[:, n:] * decay).astype(bf16)
    qg = q * eg
    kg = k * jnp.exp(gtot_c - g_c)
    wide = (nb, n, H_B * DH_B)
    blk = (lax.broadcasted_iota(jnp.int32, wide, 1) // c) == (lax.broadcasted_iota(jnp.int32, wide, 2) // DH_B)
    spread = lambda m: jnp.where(blk, jnp.concatenate([m] * H_B, axis=2), 0.0).astype(bf16)
    tall = (nb, H_B * DH_B, n)
    blk_t = (lax.broadcasted_iota(jnp.int32, tall, 1) // DH_B) == (lax.broadcasted_iota(jnp.int32, tall, 2) // c)
    kg_t = jnp.stack([kg[i].T for i in range(nb)])
    kgt = jnp.where(blk_t, jnp.concatenate([kg_t] * H_B, axis=1), 0.0).astype(bf16)
    gl = jnp.exp(jnp.sum(jnp.where(blk_t, la_r, 0.0), axis=2, keepdims=True))
    return u0, jnp.concatenate([spread(w), spread(qg)], axis=1), jnp.concatenate([aqk, kgt], axis=1), gl


def _gdn_chunk_step(prep, idx, s_st):
    pick = lambda t: jnp.stack([t[i] for i in idx])
    u0, wq_bd, ak, gl = (pick(t) for t in prep)
    n = H_B * GDN_CHUNK
    ws_qs = _bmm(wq_bd, s_st)
    ub = (u0 - ws_qs[:, :n]).astype(bf16)
    au_ku = _bmm(ak, ub)
    o = ws_qs[:, n:] + au_ku[:, :n]
    return o, gl * s_st + au_ku[:, n:]


def _gdn_kernel(xf_ref, xb_ref, grf_ref, grb_ref, prm_ref, s0_ref, of_ref, ob_ref, s_ref, *, chunks):
    @pl.when(pl.program_id(1) == 0)
    def _():
        s_ref[...] = s0_ref[...]

    c = GDN_CHUNK
    rows = lambda j: slice(j * c, (j + 1) * c)
    x = jnp.stack([xf_ref[0, rows(j)] for j in range(chunks)] + [xb_ref[0, rows(j)] for j in range(chunks)])
    graw = jnp.stack([grf_ref[0, j, 0:2] for j in range(chunks)] + [grb_ref[0, j, 2:4] for j in range(chunks)])
    prm = jnp.stack([prm_ref[0:2]] * chunks + [prm_ref[2:4]] * chunks)
    prep = _gdn_chunk_prep(x, graw, prm, chunks)
    s = s_ref[0]
    heads_to_lanes = lambda o: jnp.concatenate([o[h * c:(h + 1) * c] for h in range(H_B)], axis=1)
    for j in range(chunks):
        jb = chunks - 1 - j
        o, s = _gdn_chunk_step(prep, (j, chunks + jb), s)
        of_ref[0, rows(j)] = heads_to_lanes(o[0]).astype(of_ref.dtype)
        ob_ref[0, rows(jb)] = heads_to_lanes(o[1]).astype(ob_ref.dtype)
    s_ref[0] = s


def gdn_scan(x, gates_row, prm, s0, chunks=2):
    b, l, _ = x.shape
    c = GDN_CHUNK
    assert l % (c * chunks) == 0
    nb = l // (c * chunks)
    rows = c * chunks
    fwd = lambda bi, i: (bi, i, 0)
    bwd = lambda bi, i: (bi, nb - 1 - i, 0)
    st_spec = pl.BlockSpec((1, 2, H_B * DH_B, DH_B), lambda bi, i: (bi, 0, 0, 0))
    g_blk = (1, chunks, 4, H_B * c)
    return pl.pallas_call(
        functools.partial(_gdn_kernel, chunks=chunks),
        out_shape=[jax.ShapeDtypeStruct((b, l, B_W), bf16), jax.ShapeDtypeStruct((b, l, B_W), bf16),
                   jax.ShapeDtypeStruct(s0.shape, f32)],
        grid=(b, nb),
        in_specs=[
            pl.BlockSpec((1, rows, 3 * B_W), fwd), pl.BlockSpec((1, rows, 3 * B_W), bwd),
            pl.BlockSpec(g_blk, lambda bi, i: (bi, i, 0, 0)),
            pl.BlockSpec(g_blk, lambda bi, i: (bi, nb - 1 - i, 0, 0)),
            pl.BlockSpec((4, H_B * c), lambda bi, i: (0, 0)),
            st_spec,
        ],
        out_specs=[pl.BlockSpec((1, rows, B_W), fwd), pl.BlockSpec((1, rows, B_W), bwd), st_spec],
        compiler_params=_cparams(("parallel", "arbitrary")),
        name="gdn_scan",
    )(x, x, gates_row, gates_row, prm, s0)


def _na_kernel(sc_ref, q_ref, k_ref, v_ref, kc_ref, vc_ref, bias_ref, o_ref, *, rb, band, rows, hp, bounded):
    i = pl.program_id(2)
    ub = jnp.clip(i * rb - WIN_R // 2, 0, rows - band)
    off = pl.multiple_of(ub * GRID_W, GRID_W)
    heads = lambda t: jnp.stack([t[:, h * DH_C:(h + 1) * DH_C] for h in range(hp)])
    q = heads(q_ref[0])
    kb = heads(k_ref[0, pl.ds(off, band * GRID_W), :])
    vb = heads(v_ref[0, pl.ds(off, band * GRID_W), :])
    kc = heads(kc_ref[0])
    vc = heads(vc_ref[0])
    bqk = lambda a, b_: jnp.einsum('hqd,hkd->hqk', a, b_, preferred_element_type=f32)
    bpv = lambda a, b_: jnp.einsum('hqk,hkd->hqd', a, b_, preferred_element_type=f32)
    s_lat = bqk(q, kb) + bias_ref[0]
    s_ctx = bqk(q, kc)
    if bounded:
        def with_ones(t):
            ones_col = jnp.where(lax.broadcasted_iota(jnp.int32, t.shape, 2) == 0, 1.0, 0.0).astype(bf16)
            return jnp.concatenate([t, ones_col], axis=2)

        bound = sc_ref[0]
        acc = (bpv(jnp.exp2(s_lat - bound).astype(bf16), with_ones(vb))
               + bpv(jnp.exp2(s_ctx - bound).astype(bf16), with_ones(vc)))
        o = acc[:, :, :DH_C] * (1.0 / jnp.sum(acc[:, :, DH_C:], axis=-1, keepdims=True))
    else:
        m = jnp.maximum(jnp.max(s_lat, axis=-1, keepdims=True), jnp.max(s_ctx, axis=-1, keepdims=True))
        p_lat = jnp.exp2(s_lat - m)
        p_ctx = jnp.exp2(s_ctx - m)
        den = jnp.sum(p_lat, axis=-1, keepdims=True) + jnp.sum(p_ctx, axis=-1, keepdims=True)
        o = (bpv(p_lat.astype(bf16), vb) + bpv(p_ctx.astype(bf16), vc)) * (1.0 / den)
    o_ref[0] = jnp.concatenate([o[h] for h in range(hp)], axis=-1).astype(o_ref.dtype)


def _na_bias_index(rows, rb, band):
    wr = min(WIN_R, rows)
    nblk = rows // rb
    cols = np.arange(GRID_W)
    c_start = np.clip(cols - WIN_C // 2, 0, GRID_W - WIN_C)
    col_ok = (cols[None, :] >= c_start[:, None]) & (cols[None, :] < c_start[:, None] + WIN_C)
    dc_idx = np.clip(cols[None, :] - cols[:, None] + WIN_C - 1, 0, 2 * WIN_C - 2)

    def geometry(i):
        ub = int(np.clip(i * rb - WIN_R // 2, 0, rows - band))
        qr = i * rb + np.arange(rb)
        kr = ub + np.arange(band)
        r_start = np.clip(qr - wr // 2, 0, rows - wr)
        row_ok = (kr[None, :] >= r_start[:, None]) & (kr[None, :] < r_start[:, None] + wr)
        dr = np.clip(kr[None, :] - qr[:, None] + WIN_R - 1, 0, 2 * WIN_R - 2)
        ok = row_ok[:, None, :, None] & col_ok[None, :, None, :]
        return ok.reshape(rb * GRID_W, band * GRID_W), dr

    reps = [0, min(1, nblk - 1), nblk - 1]
    geo = [geometry(i) for i in reps]
    for i in range(1, nblk - 1):
        g = geometry(i)
        assert all(np.array_equal(a, b_) for a, b_ in zip(g, geo[1]))
    ok = np.stack([g[0] for g in geo])
    dr = np.stack([g[1] for g in geo])
    dr_onehot = (dr[..., None] == np.arange(2 * WIN_R - 1)).astype(np.float32)
    dc_onehot = (dc_idx[None] == np.arange(2 * WIN_C - 1)[:, None, None]).astype(np.float32)
    return ok, dr_onehot, dc_onehot


def _na_bias_table(rpb, rows, rb, band):
    ok, dr_onehot, dc_onehot = _na_bias_index(rows, rb, band)
    hi = lax.Precision.HIGHEST
    by_col = jnp.einsum('hrc,cqk->hrqk', rpb.astype(f32), dc_onehot, precision=hi)
    slabs = jnp.einsum('vabr,hrqk->vhaqbk', dr_onehot, by_col, precision=hi)
    h = rpb.shape[0]
    return jnp.where(ok[:, None], slabs.reshape(3, h, rb * GRID_W, band * GRID_W) * math.log2(math.e), NEG)


def neighbourhood_attention(q, k, v, k_ctx, v_ctx, rpb, bound, bounded, rb=4):
    b, l, _ = q.shape
    lc = k_ctx.shape[1]
    rows = l // GRID_W
    band = rb + WIN_R - 1
    assert rows % rb == 0 and rows >= band and WIN_R <= rows
    nblk = rows // rb
    bias = _na_bias_table(rpb, rows, rb, band)
    tq = rb * GRID_W
    variant = lambda i: jnp.where(i == 0, 0, jnp.where(i == nblk - 1, 2, 1))
    hp = 4
    assert H_C % hp == 0
    hw = hp * DH_C
    head_blk = lambda n: pl.BlockSpec((1, n, hw), lambda bi, h, i: (bi, 0, h))
    return pl.pallas_call(
        functools.partial(_na_kernel, rb=rb, band=band, rows=rows, hp=hp, bounded=bounded),
        out_shape=jax.ShapeDtypeStruct((b, l, H_C * DH_C), bf16),
        grid=(b, H_C // hp, nblk),
        in_specs=[
            pl.BlockSpec(memory_space=pltpu.SMEM),
            pl.BlockSpec((1, tq, hw), lambda bi, h, i: (bi, i, h)),
            head_blk(l), head_blk(l), head_blk(lc), head_blk(lc),
            pl.BlockSpec((1, hp, tq, band * GRID_W), lambda bi, h, i: (variant(i), h, 0, 0)),
        ],
        out_specs=pl.BlockSpec((1, tq, hw), lambda bi, h, i: (bi, i, h)),
        compiler_params=_cparams(("parallel", "parallel", "arbitrary")),
        name="na_attn" if bounded else "na_attn_online",
    )(bound, q, k, v, k_ctx, v_ctx, bias)


def _outproj_kernel(*refs, has_gdn):
    if has_gdn:
        x_ref, g_ref, a_ref, wa_ref, of_ref, ob_ref, gate_ref, gain_ref, wb_ref, o_ref = refs
    else:
        x_ref, g_ref, a_ref, wa_ref, o_ref = refs
    acc = _dot(a_ref[0], wa_ref[...])
    if has_gdn:
        o = of_ref[0].astype(f32) + ob_ref[0].astype(f32)
        gate = gate_ref[0].astype(f32)
        parts = []
        for h in range(H_B):
            sl = slice(h * DH_B, (h + 1) * DH_B)
            t = o[:, sl]
            y = t * lax.rsqrt(jnp.mean(t * t, axis=-1, keepdims=True) + EPS) * gain_ref[...]
            parts.append((y * _silu(gate[:, sl])).astype(bf16))
        acc = acc + _dot(jnp.concatenate(parts, axis=-1), wb_ref[...])
    o_ref[0] = x_ref[0] + g_ref[0] * acc


def out_project(x, g, a, wa, gdn=None, tm=512, name="outproj"):
    b, l, d = x.shape
    tm = min(tm, l)
    assert l % tm == 0
    bm = g.shape[0]
    mod_map = (lambda bi, i: (bi, 0, 0)) if bm == b else (lambda bi, i: (0, 0, 0))
    row = lambda n: pl.BlockSpec((1, tm, n), lambda bi, i: (bi, i, 0))
    full = lambda arr: pl.BlockSpec(arr.shape, lambda bi, i: (0, 0))
    in_specs = [row(d), pl.BlockSpec((1, 1, d), mod_map), row(a.shape[2]), full(wa)]
    args = [x, g, a, wa]
    if gdn is not None:
        o_f, o_b, gate, gain, wb = gdn
        in_specs += [row(B_W), row(B_W), row(B_W), full(gain), full(wb)]
        args += [o_f, o_b, gate, gain, wb]
    return pl.pallas_call(
        functools.partial(_outproj_kernel, has_gdn=gdn is not None),
        out_shape=jax.ShapeDtypeStruct((b, l, d), f32),
        grid=(b, l // tm),
        in_specs=in_specs,
        out_specs=row(d),
        compiler_params=_cparams(("parallel", "parallel")),
        name=name,
    )(*args)


def _ffn_kernel(x_ref, gain_ref, sc_ref, sh_ref, g_ref, win_ref, wo_ref, o_ref, h_ref, acc_ref, *, tf):
    h_ref[...] = _norm_mod(x_ref[0], gain_ref[...], sc_ref[0], sh_ref[0]).astype(bf16)
    acc_ref[...] = jnp.zeros_like(acc_ref)
    ff = wo_ref.shape[0]
    tm = h_ref.shape[0]
    halves = [slice(r * (tm // 2), (r + 1) * (tm // 2)) for r in range(2)] if tm >= 512 else [slice(0, tm)]

    def body(j, carry):
        c0 = pl.multiple_of(j * tf, tf)
        wg = win_ref[:, pl.ds(c0, tf)]
        wu = win_ref[:, pl.ds(ff + c0, tf)]
        wo = wo_ref[pl.ds(c0, tf), :]
        gate_up = []
        for rows in halves:
            h = h_ref[rows, :]
            gate_up.append((_dot(h, wg), _dot(h, wu)))
        for rows, (gt, up) in zip(halves, gate_up):
            acc_ref[rows, :] += _dot((_silu(gt) * up).astype(bf16), wo)
        return carry

    lax.fori_loop(0, ff // tf, body, 0)
    o_ref[0] = x_ref[0] + g_ref[0] * acc_ref[...]


def ffn(x, gain, sc, sh, g, w_in, w_out, tm=1024, tf=256, name="ffn"):
    b, l, d = x.shape
    ff = w_out.shape[0]
    tm = min(tm, l)
    assert l % tm == 0 and ff % tf == 0
    bm = sc.shape[0]
    mod_map = (lambda bi, i: (bi, 0, 0)) if bm == b else (lambda bi, i: (0, 0, 0))
    mod = pl.BlockSpec((1, 1, d), mod_map)
    resident = lambda arr: pl.BlockSpec(arr.shape, lambda bi, i: (0, 0), pipeline_mode=pl.Buffered(1))
    return pl.pallas_call(
        functools.partial(_ffn_kernel, tf=tf),
        out_shape=jax.ShapeDtypeStruct((b, l, d), f32),
        grid=(b, l // tm),
        in_specs=[
            pl.BlockSpec((1, tm, d), lambda bi, i: (bi, i, 0)),
            pl.BlockSpec((1, d), lambda bi, i: (0, 0)),
            mod, mod, mod,
            resident(w_in), resident(w_out),
        ],
        out_specs=pl.BlockSpec((1, tm, d), lambda bi, i: (bi, i, 0)),
        scratch_shapes=[pltpu.VMEM((tm, d), bf16), pltpu.VMEM((tm, d), f32)],
        compiler_params=_cparams(("parallel", "parallel")),
        name=name,
    )(x, gain, sc, sh, g, w_in, w_out)


def _rope_tables(l):
    n_freq = DH_A // 4
    inv = ROPE_BASE ** (-jnp.arange(n_freq, dtype=f32) / n_freq)
    t = jnp.arange(l)
    row = (t // GRID_W).astype(f32)
    col = (t % GRID_W).astype(f32)
    ang = jnp.concatenate([row[:, None] * inv, col[:, None] * inv], axis=-1)
    cos = jnp.repeat(jnp.cos(ang), 2, axis=-1)
    sin = jnp.repeat(jnp.sin(ang), 2, axis=-1)
    sign = jnp.tile(jnp.array([-1.0, 1.0], f32), DH_A // 2)
    return jnp.tile(cos, (1, 2)), jnp.tile(sin * sign, (1, 2))


def _pad_rows(a, rows):
    return jnp.zeros((rows,) + a.shape[1:], a.dtype).at[:a.shape[0]].set(a)


def kernel(x, c, ctx, c_ctx, ada_w, ada_b, norm_mix, norm_ffn, ffn_w_in, ffn_w_out, even_w_in, even_w_out,
           diff_qk_gain, diff_lambda, diff_subln, gdn_conv, gdn_a_log, gdn_dt_bias, gdn_norm, odd_w_in,
           odd_w_out, na_qk_gain, na_rpb):
    b, l, d = x.shape
    lc = ctx.shape[1]
    depth = ada_w.shape[0]
    c_all = _pad_rows(jnp.concatenate([c, c_ctx[None]], axis=0), 16)
    mods = ada_modulation(c_all, ada_w, ada_b)
    cos_t, sin_t = _rope_tables(l)
    ctx_flat = ctx.reshape(1, b * lc, d)

    for layer in range(depth):
        ctx_out = layer < depth - 1
        m_lat = [t[:, None, :] for t in jnp.split(mods[layer, :b], 6, axis=-1)]
        m_ctx = [t[:, None, :] for t in jnp.split(mods[layer, b:b + 1], 6, axis=-1)]
        sh_m, sc_m, g_m, sh_f, sc_f, g_f = m_lat
        csh_m, csc_m, cg_m, csh_f, csc_f, cg_f = m_ctx
        gain_m = norm_mix[layer][None]
        gain_f = norm_ffn[layer][None]
        if layer % 2 == 0:
            e = layer // 2
            lam_init = 0.8 - 0.6 * math.exp(-0.3 * layer)
            w_in = even_w_in[e].astype(bf16)
            cuts = [0, A_W, 2 * A_W, 3 * A_W, 3 * A_W + 3 * B_W, 3 * A_W + 4 * B_W]
            ws = [w_in[:, cuts[i]:cuts[i + 1]] for i in range(5)]
            w_gates = jnp.zeros((d, LANES), bf16).at[:, :4 * H_B].set(w_in[:, cuts[5]:])
            ws.append(w_gates)
            qk_gain = _pad_rows(jnp.tile(diff_qk_gain[e], (1, 2)), 8)
            dts = [bf16, bf16, bf16, f32, bf16, f32]
            q_scale = DH_A ** -0.5 * math.log2(math.e)
            epi = lambda rope: [("submap", 0, rope, q_scale), ("submap", 1, rope, 1.0), ("plain",),
                                ("plain",), ("plain",), ("plain",)]
            qa, ka, va, qkv_b, g_b, gates = norm_mod_project(
                x, gain_m, sc_m, sh_m, ws, dts, epi(True), qk_gain, rope_tabs=(cos_t, sin_t), name="even_proj")
            qac, kac, vac, qkv_bc, g_bc, gates_c = norm_mod_project(
                ctx_flat, gain_m, csc_m, csh_m, ws, dts, epi(False), qk_gain, name="even_proj_ctx")
            unflat = lambda t: t.reshape(b, lc, t.shape[-1])
            qac, kac, vac, qkv_bc, g_bc, gates_c = map(unflat, (qac, kac, vac, qkv_bc, g_bc, gates_c))

            lv = diff_lambda[e]
            lam = jnp.exp(jnp.sum(lv[0] * lv[1])) - jnp.exp(jnp.sum(lv[2] * lv[3])) + lam_init
            gmax = jnp.max(jnp.abs(diff_qk_gain[e]), axis=-1)
            bound = 1.02 * DH_A * q_scale * gmax[0] * gmax[1]
            scal = jnp.stack([lam, bound]).astype(f32)
            subln = diff_subln[e][None]
            attn = functools.partial(diff_attention, scal, qa, kac, vac, ka, va, subln, lam_init)
            a_lat = lax.cond(2.0 * bound < 120.0,
                             lambda: attn(bounded=True, tk=4096, name="diff_attn"),
                             lambda: attn(bounded=False, name="diff_attn_online"))
            a_ctx = diff_attention(scal, qac, kac, vac, None, None, subln, lam_init, name="diff_attn_ctx")

            y_lat = gdn_short_conv(qkv_b, gdn_conv[e])
            y_ctx = gdn_short_conv(qkv_bc, gdn_conv[e])
            per_pos = lambda p: jnp.repeat(p, GDN_CHUNK)
            prm = jnp.stack([per_pos(gdn_a_log[e][0]), per_pos(gdn_dt_bias[e][0]),
                             per_pos(gdn_a_log[e][1]), per_pos(gdn_dt_bias[e][1])])

            def to_rows(g):
                t = g[:, :, :4 * H_B].reshape(g.shape[0], -1, GDN_CHUNK, 4, H_B).transpose(0, 1, 3, 4, 2)
                t = t.reshape(g.shape[0], -1, 4, H_B * GDN_CHUNK)
                return t[:, :, jnp.array([0, 2, 1, 3])]

            s0 = jnp.zeros((b, 2, H_B * DH_B, DH_B), f32)
            ocf, ocb, s_mid = gdn_scan(y_ctx, to_rows(gates_c), prm, s0)
            olf, olb, _ = gdn_scan(y_lat, to_rows(gates), prm, s_mid)

            w_out = even_w_out[e].astype(bf16)
            gdn_gain = gdn_norm[e][None]
            x = out_project(x, g_m, a_lat, w_out[:A_W], gdn=(olf, olb, g_b, gdn_gain, w_out[A_W:]),
                            name="even_out")
            ctx_flat = out_project(
                ctx_flat, cg_m, a_ctx.reshape(1, b * lc, A_W), w_out[:A_W],
                gdn=(ocf.reshape(1, b * lc, B_W), ocb.reshape(1, b * lc, B_W), g_bc.reshape(1, b * lc, B_W),
                     gdn_gain, w_out[A_W:]), name="even_out_ctx")
        else:
            od = layer // 2
            w_in = odd_w_in[od].astype(bf16)
            mix = H_C * DH_C
            ws = [w_in[:, :mix], w_in[:, mix:2 * mix], w_in[:, 2 * mix:]]
            qk_gain = _pad_rows(na_qk_gain[od], 8)
            q_scale = DH_C ** -0.5 * math.log2(math.e)
            q, k, v = norm_mod_project(
                x, gain_m, sc_m, sh_m, ws, [bf16] * 3,
                [("headnorm", 0, q_scale), ("headnorm", 1, 1.0), ("plain",)], qk_gain, name="odd_proj")
            kc, vc = norm_mod_project(
                ctx_flat, gain_m, csc_m, csh_m, ws[1:], [bf16] * 2,
                [("headnorm", 1, 1.0), ("plain",)], qk_gain, name="odd_proj_ctx")
            kc, vc = kc.reshape(b, lc, mix), vc.reshape(b, lc, mix)
            gmax = jnp.max(jnp.abs(na_qk_gain[od]), axis=-1)
            bound = (1.02 * DH_C * q_scale * gmax[0] * gmax[1]
                     + jnp.max(jnp.abs(na_rpb[od])) * math.log2(math.e)).astype(f32).reshape(1)
            na = functools.partial(neighbourhood_attention, q, k, v, kc, vc, na_rpb[od], bound)
            o = lax.cond(2.0 * bound[0] < 120.0, lambda: na(True), lambda: na(False))
            x = out_project(x, g_m, o, odd_w_out[od].astype(bf16), name="odd_out")
            if ctx_out:
                raise NotImplementedError("context output of a neighbourhood layer is not needed at depth 2")

        wf_in = ffn_w_in[layer].astype(bf16)
        wf_out = ffn_w_out[layer].astype(bf16)
        x = ffn(x, gain_f, sc_f, sh_f, g_f, wf_in, wf_out, name="ffn")
        if ctx_out:
            ctx_flat = ffn(ctx_flat, gain_f, csc_f, csh_f, cg_f, wf_in, wf_out, name="ffn_ctx")
    return x
```

```python
import functools
import math

import numpy as np
import jax
import jax.numpy as jnp
from jax import lax
from jax.experimental import pallas as pl
from jax.experimental.pallas import tpu as pltpu

f32 = jnp.float32
bf16 = jnp.bfloat16

EPS = 1e-6
ROPE_BASE = 10000.0
GRID_W = 64
H_A = 4
DH_A = 64
DV_A = 2 * DH_A
A_W = H_A * DV_A
H_B = 4
DH_B = 128
B_W = H_B * DH_B
GDN_CHUNK = 64
CONV_K = 5
H_C = 8
DH_C = 128
WIN_R = 8
WIN_C = 16

LANES = 128
VMEM_LIMIT = 56 * 1024 * 1024
NEG = -1e30


def _cparams(sem):
    return pltpu.CompilerParams(dimension_semantics=sem, vmem_limit_bytes=VMEM_LIMIT)


def _silu(x):
    return x * (1.0 / (1.0 + jnp.exp(-x)))


def _dot(a, b):
    return jnp.dot(a, b, preferred_element_type=f32)


def _dot_nt(a, b):
    return lax.dot_general(a, b, (((1,), (1,)), ((), ())), preferred_element_type=f32)


def _split_bf16(x):
    hi = x.astype(bf16)
    lo = (x - hi.astype(f32)).astype(bf16)
    return hi, lo


def _dot3(a, b):
    ah, al = _split_bf16(a)
    bh, bl = _split_bf16(b)
    return _dot(ah, bh) + (_dot(ah, bl) + _dot(al, bh))


def _mm(a, b):
    return _dot(a.astype(bf16), b.astype(bf16))


def _dot3_nt(a, b):
    ah, al = _split_bf16(a)
    bh, bl = _split_bf16(b)
    return _dot_nt(ah, bh) + (_dot_nt(ah, bl) + _dot_nt(al, bh))


def _ada_kernel(c_ref, w_ref, b_ref, o_ref):
    sc = _silu(c_ref[...])
    o_ref[0] = _dot3(sc, w_ref[0]) + b_ref[0]


def ada_modulation(c_all, ada_w, ada_b):
    depth, d, n = ada_w.shape
    rows = c_all.shape[0]
    tn = 1536
    return pl.pallas_call(
        _ada_kernel,
        out_shape=jax.ShapeDtypeStruct((depth, rows, n), f32),
        grid=(depth, n // tn),
        in_specs=[
            pl.BlockSpec((rows, d), lambda l, j: (0, 0)),
            pl.BlockSpec((1, d, tn), lambda l, j: (l, 0, j)),
            pl.BlockSpec((1, 1, tn), lambda l, j: (l, 0, j)),
        ],
        out_specs=pl.BlockSpec((1, rows, tn), lambda l, j: (l, 0, j)),
        compiler_params=_cparams(("parallel", "parallel")),
        name="ada_modulation",
    )(c_all, ada_w, ada_b.reshape(depth, 1, n))


def _norm_mod(x, gain, sc, sh):
    y = x * lax.rsqrt(jnp.mean(x * x, axis=-1, keepdims=True) + EPS)
    return (y * gain) * (1.0 + sc) + sh


def _swap_pairs(x):
    lane = lax.broadcasted_iota(jnp.int32, x.shape, x.ndim - 1)
    nxt = pltpu.roll(x, x.shape[-1] - 1, x.ndim - 1)
    prv = pltpu.roll(x, 1, x.ndim - 1)
    return jnp.where(lane % 2 == 0, nxt, prv)


def _submap_norm_rope(t, gain, cos, sin, scale):
    lane = lax.broadcasted_iota(jnp.int32, t.shape, 1)
    lo = lane < DH_A
    sq = t * t
    s_lo = jnp.sum(jnp.where(lo, sq, 0.0), axis=-1, keepdims=True)
    s_hi = jnp.sum(jnp.where(lo, 0.0, sq), axis=-1, keepdims=True)
    r = jnp.where(lo, lax.rsqrt(s_lo * (1.0 / DH_A) + EPS), lax.rsqrt(s_hi * (1.0 / DH_A) + EPS))
    y = t * r * gain
    if cos is not None:
        y = y * cos + _swap_pairs(y) * sin
    if scale != 1.0:
        y = y * scale
    return y


def _head_norm(t, gain, scale):
    y = t * lax.rsqrt(jnp.mean(t * t, axis=-1, keepdims=True) + EPS) * gain
    if scale != 1.0:
        y = y * scale
    return y


def _proj_kernel(*refs, epilogues, rope):
    n_out = len(epilogues)
    x_ref, gain_ref, sc_ref, sh_ref = refs[:4]
    pos = 4
    if rope:
        cos_ref, sin_ref = refs[pos:pos + 2]
        pos += 2
    qkg_ref = refs[pos]
    pos += 1
    w_refs = refs[pos:pos + n_out]
    o_refs = refs[pos + n_out:pos + 2 * n_out]

    h = _norm_mod(x_ref[0], gain_ref[...], sc_ref[0], sh_ref[0]).astype(bf16)
    for w_ref, o_ref, epi in zip(w_refs, o_refs, epilogues):
        acc = _dot(h, w_ref[...])
        kind = epi[0]
        if kind == "plain":
            o_ref[0] = acc.astype(o_ref.dtype)
        elif kind == "submap":
            _, grow, use_rope, scale = epi
            gain = qkg_ref[grow:grow + 1, :]
            for hd in range(acc.shape[1] // LANES):
                t = acc[:, hd * LANES:(hd + 1) * LANES]
                cs = (cos_ref[...], sin_ref[...]) if (rope and use_rope) else (None, None)
                y = _submap_norm_rope(t, gain, cs[0], cs[1], scale)
                o_ref[0, :, hd * LANES:(hd + 1) * LANES] = y.astype(o_ref.dtype)
        elif kind == "headnorm":
            _, grow, scale = epi
            gain = qkg_ref[grow:grow + 1, :]
            for hd in range(acc.shape[1] // LANES):
                t = acc[:, hd * LANES:(hd + 1) * LANES]
                o_ref[0, :, hd * LANES:(hd + 1) * LANES] = _head_norm(t, gain, scale).astype(o_ref.dtype)
        else:
            raise ValueError(kind)


def norm_mod_project(x, gain, sc, sh, weights, out_dtypes, epilogues, qk_gain, rope_tabs=None,
                     tm=512, name="proj"):
    b, l, d = x.shape
    tm = min(tm, l)
    assert l % tm == 0
    bm = sc.shape[0]
    mod_map = (lambda bi, i: (bi, 0, 0)) if bm == b else (lambda bi, i: (0, 0, 0))
    rope = rope_tabs is not None
    in_specs = [
        pl.BlockSpec((1, tm, d), lambda bi, i: (bi, i, 0)),
        pl.BlockSpec((1, d), lambda bi, i: (0, 0)),
        pl.BlockSpec((1, 1, d), mod_map),
        pl.BlockSpec((1, 1, d), mod_map),
    ]
    args = [x, gain, sc, sh]
    if rope:
        in_specs += [pl.BlockSpec((tm, LANES), lambda bi, i: (i, 0))] * 2
        args += list(rope_tabs)
    in_specs.append(pl.BlockSpec(qk_gain.shape, lambda bi, i: (0, 0)))
    args.append(qk_gain)
    for w in weights:
        in_specs.append(pl.BlockSpec(w.shape, lambda bi, i: (0, 0)))
        args.append(w)
    out_shape = [jax.ShapeDtypeStruct((b, l, w.shape[1]), dt) for w, dt in zip(weights, out_dtypes)]
    out_specs = [pl.BlockSpec((1, tm, w.shape[1]), lambda bi, i: (bi, i, 0)) for w in weights]
    return pl.pallas_call(
        functools.partial(_proj_kernel, epilogues=tuple(epilogues), rope=rope),
        out_shape=out_shape,
        grid=(b, l // tm),
        in_specs=in_specs,
        out_specs=out_specs,
        compiler_params=_cparams(("parallel", "parallel")),
        name=name,
    )(*args)


def _diff_attn_kernel(*refs, has_lat, tk, lam_init):
    if has_lat:
        sc_ref, q_ref, kc_ref, vc_ref, k_ref, v_ref, subln_ref, o_ref = refs
    else:
        sc_ref, q_ref, kc_ref, vc_ref, subln_ref, o_ref = refs
    q = q_ref[0]
    tq = q.shape[0]
    lane = lax.broadcasted_iota(jnp.int32, q.shape, 1)
    zero = jnp.zeros_like(q)
    q1 = jnp.where(lane < DH_A, q, zero)
    q2 = jnp.where(lane < DH_A, zero, q)

    def kv_loop(step, carry):
        carry = step(kc_ref[0], vc_ref[0], carry)
        if has_lat:
            def body(j, c):
                off = pl.multiple_of(j * tk, tk)
                return step(k_ref[0, pl.ds(off, tk), :], v_ref[0, pl.ds(off, tk), :], c)

            carry = lax.fori_loop(0, k_ref.shape[1] // tk, body, carry)
        return carry

    def one_map(qm, kblk, vblk, m, l, acc):
        s = _dot_nt(qm, kblk)
        m_new = jnp.maximum(m, jnp.max(s, axis=-1, keepdims=True))
        p = jnp.exp2(s - m_new)
        a = jnp.exp2(m - m_new)
        l = a * l + jnp.sum(p, axis=-1, keepdims=True)
        acc = a * acc + _dot(p.astype(bf16), vblk)
        return m_new, l, acc

    def step(kblk, vblk, carry):
        m1, l1, a1, m2, l2, a2 = carry
        m1, l1, a1 = one_map(q1, kblk, vblk, m1, l1, a1)
        m2, l2, a2 = one_map(q2, kblk, vblk, m2, l2, a2)
        return m1, l1, a1, m2, l2, a2

    mi = jnp.full((tq, 1), NEG, f32)
    li = jnp.zeros((tq, 1), f32)
    ai = jnp.zeros((tq, DV_A), f32)
    _, l1, a1, _, l2, a2 = kv_loop(step, (mi, li, ai, mi, li, ai))
    lam = sc_ref[0]
    o = a1 * (1.0 / l1) - lam * (a2 * (1.0 / l2))
    y = o * lax.rsqrt(jnp.mean(o * o, axis=-1, keepdims=True) + EPS) * subln_ref[...]
    o_ref[0] = (y * (1.0 - lam_init)).astype(o_ref.dtype)


def diff_attention(lam, q, k_ctx, v_ctx, k_lat, v_lat, subln, lam_init, tq=512, tk=512, name="diff_attn"):
    b, lq, _ = q.shape
    lc = k_ctx.shape[1]
    has_lat = k_lat is not None
    tq = min(tq, lq)
    assert lq % tq == 0
    head_blk = lambda rows: pl.BlockSpec((1, rows, LANES), lambda bi, h, i: (bi, 0, h))
    in_specs = [
        pl.BlockSpec(memory_space=pltpu.SMEM),
        pl.BlockSpec((1, tq, LANES), lambda bi, h, i: (bi, i, h)),
        head_blk(lc), head_blk(lc),
    ]
    args = [lam, q, k_ctx, v_ctx]
    if has_lat:
        ll = k_lat.shape[1]
        tk = min(tk, ll)
        assert ll % tk == 0
        in_specs += [head_blk(ll), head_blk(ll)]
        args += [k_lat, v_lat]
    in_specs.append(pl.BlockSpec((1, LANES), lambda bi, h, i: (0, 0)))
    args.append(subln)
    return pl.pallas_call(
        functools.partial(_diff_attn_kernel, has_lat=has_lat, tk=tk, lam_init=lam_init),
        out_shape=jax.ShapeDtypeStruct((b, lq, A_W), bf16),
        grid=(b, H_A, lq // tq),
        in_specs=in_specs,
        out_specs=pl.BlockSpec((1, tq, LANES), lambda bi, h, i: (bi, i, h)),
        compiler_params=_cparams(("parallel", "parallel", "arbitrary")),
        name=name,
    )(*args)


def _diff_attn_t_kernel(sc_ref, qt_ref, kc_ref, vct_ref, k_ref, vt_ref, subln_ref, o_ref, *, tk, lam_init):
    qt = qt_ref[0, 0]
    tq = qt.shape[1]
    row = lax.broadcasted_iota(jnp.int32, qt.shape, 0)
    zero = jnp.zeros_like(qt)
    q12t = jnp.concatenate([jnp.where(row < DH_A, qt, zero), jnp.where(row < DH_A, zero, qt)], axis=1)
    bound = sc_ref[1]

    def step(kblk, vtblk, acc):
        return acc + _dot(vtblk, jnp.exp2(_dot(kblk, q12t) - bound).astype(bf16))

    acc = step(kc_ref[0], vct_ref[0, 0], jnp.zeros((vt_ref.shape[2], 2 * tq), f32))

    def body(j, a):
        off = pl.multiple_of(j * tk, tk)
        return step(k_ref[0, pl.ds(off, tk), :], vt_ref[0, 0, :, pl.ds(off, tk)], a)

    acc = lax.fori_loop(0, k_ref.shape[1] // tk, body, acc)
    inv = 1.0 / acc[DV_A:DV_A + 1, :]
    a = acc[:DV_A, :] * inv
    o = (a[:, :tq] - sc_ref[0] * a[:, tq:]).T
    y = o * lax.rsqrt(jnp.mean(o * o, axis=-1, keepdims=True) + EPS) * subln_ref[...]
    o_ref[0] = (y * (1.0 - lam_init)).astype(o_ref.dtype)


V_AUG_ROWS = DV_A + 16


def _heads_transposed(t, ones_row):
    b, l, _ = t.shape
    tt = t.reshape(b, l, H_A, DV_A).transpose(0, 2, 3, 1)
    if not ones_row:
        return tt
    extra = jnp.zeros((b, H_A, V_AUG_ROWS - DV_A, l), t.dtype).at[:, :, 0].set(1)
    return jnp.concatenate([tt, extra], axis=2)


def diff_attention_bounded(scal, q, k_ctx, v_ctx, k_lat, v_lat, subln, lam_init, tq=512, tk=4096,
                           name="diff_attn"):
    b, lq, _ = q.shape
    lc, ll = k_ctx.shape[1], k_lat.shape[1]
    tq, tk = min(tq, lq), min(tk, ll)
    assert lq % tq == 0 and ll % tk == 0
    qt = _heads_transposed(q, False)
    vct = _heads_transposed(v_ctx, True)
    vt = _heads_transposed(v_lat, True)
    return pl.pallas_call(
        functools.partial(_diff_attn_t_kernel, tk=tk, lam_init=lam_init),
        out_shape=jax.ShapeDtypeStruct((b, lq, A_W), bf16),
        grid=(b, H_A, lq // tq),
        in_specs=[
            pl.BlockSpec(memory_space=pltpu.SMEM),
            pl.BlockSpec((1, 1, DV_A, tq), lambda bi, h, i: (bi, h, 0, i)),
            pl.BlockSpec((1, lc, LANES), lambda bi, h, i: (bi, 0, h)),
            pl.BlockSpec((1, 1, V_AUG_ROWS, lc), lambda bi, h, i: (bi, h, 0, 0)),
            pl.BlockSpec((1, ll, LANES), lambda bi, h, i: (bi, 0, h)),
            pl.BlockSpec((1, 1, V_AUG_ROWS, ll), lambda bi, h, i: (bi, h, 0, 0)),
            pl.BlockSpec((1, LANES), lambda bi, h, i: (0, 0)),
        ],
        out_specs=pl.BlockSpec((1, tq, LANES), lambda bi, h, i: (bi, i, h)),
        compiler_params=_cparams(("parallel", "parallel", "arbitrary")),
        name=name,
    )(scal, qt, k_ctx, vct, k_lat, vt, subln)


def _gdn_conv_kernel(x_ref, w_ref, o_ref, pad_ref, *, rows_per_step):
    l = x_ref.shape[1]
    half = CONV_K // 2
    halo = 8
    zeros = jnp.zeros((halo, LANES), f32)
    pad_ref[0:halo, :] = zeros
    pad_ref[halo + l:halo + l + halo, :] = zeros
    pad_ref[halo:halo + l, :] = x_ref[0]
    kind = pl.program_id(1) // H_B
    w = w_ref[...]
    r = rows_per_step

    def body(i, _):
        t0 = pl.multiple_of(i * r, r)
        y = jnp.zeros((r, LANES), f32)
        for j in range(CONV_K):
            y = y + pad_ref[pl.ds(t0 + (halo - half + j), r), :] * w[j:j + 1, :]
        y = _silu(y)
        nrm = lax.rsqrt(jnp.sum(y * y, axis=-1, keepdims=True) + EPS)
        nrm = jnp.where(kind == 0, nrm * (DH_B ** -0.5), nrm)
        y = jnp.where(kind == 2, y, y * nrm)
        o_ref[0, pl.ds(t0, r), :] = y
        return 0

    lax.fori_loop(0, l // r, body, 0)


def gdn_short_conv(qkv, conv_w):
    b, l, c = qkv.shape
    r = min(512, l)
    assert l % r == 0
    wpad = jnp.zeros((8, c), f32).at[:CONV_K].set(conv_w)
    return pl.pallas_call(
        functools.partial(_gdn_conv_kernel, rows_per_step=r),
        out_shape=jax.ShapeDtypeStruct((b, l, c), f32),
        grid=(b, c // LANES),
        in_specs=[
            pl.BlockSpec((1, l, LANES), lambda bi, j: (bi, 0, j)),
            pl.BlockSpec((8, LANES), lambda bi, j: (0, j)),
        ],
        out_specs=pl.BlockSpec((1, l, LANES), lambda bi, j: (bi, 0, j)),
        scratch_shapes=[pltpu.VMEM((l + 16, LANES), f32)],
        compiler_params=_cparams(("parallel", "parallel")),
        name="gdn_conv",
    )(qkv, wpad)


def _softplus(x):
    return jnp.maximum(x, 0.0) + jnp.log1p(jnp.exp(-jnp.abs(x)))


def _sigmoid(x):
    return 1.0 / (1.0 + jnp.exp(-x))


def _stack_heads(x, base):
    return jnp.concatenate([x[:, base + h * DH_B:base + (h + 1) * DH_B] for h in range(H_B)], axis=0)


def _bmm(a, b):
    return jnp.einsum('bij,bjk->bik', a.astype(bf16), b.astype(bf16), preferred_element_type=f32)


def _bmm_nt(a, b):
    return jnp.einsum('bik,bjk->bij', a.astype(bf16), b.astype(bf16), preferred_element_type=f32)


def _gdn_chunk_prep(x, graw, prm, n_fwd):
    c = GDN_CHUNK
    n = H_B * c
    nb = x.shape[0]
    stack = lambda base: jnp.concatenate(
        [x[:, :, base + h * DH_B:base + (h + 1) * DH_B] for h in range(H_B)], axis=1)
    q, k, v = stack(0), stack(B_W), stack(2 * B_W)
    shp = (nb, n, n)
    chain = lax.broadcasted_iota(jnp.int32, shp, 0)
    r = lax.broadcasted_iota(jnp.int32, shp, 1)
    cc = lax.broadcasted_iota(jnp.int32, shp, 2)
    same = (r // c) == (cc // c)
    eye = r == cc
    ahead = jnp.where(chain < n_fwd, r - cc, cc - r)
    far = jnp.int32(4 * n)
    strict = jnp.where(same, ahead, -far) > 0
    incl = jnp.where(same, ahead, -far) >= 0
    incl_t = jnp.where(same, ahead, far) <= 0

    beta_r = _sigmoid(graw[:, 0:1, :])
    la_r = -jnp.exp(prm[:, 0:1, :]) * _softplus(graw[:, 1:2, :] + prm[:, 1:2, :])
    to_col = lambda row: jnp.sum(jnp.where(eye, row, 0.0), axis=2, keepdims=True)
    beta_c = to_col(beta_r)
    la_c = to_col(la_r)
    g_c = jnp.sum(jnp.where(incl, la_r, 0.0), axis=2, keepdims=True)
    g_r = jnp.sum(jnp.where(incl_t, la_c, 0.0), axis=1, keepdims=True)
    gtot_c = jnp.sum(jnp.where(same, la_r, 0.0), axis=2, keepdims=True)
    decay = jnp.where(incl, jnp.exp(jnp.where(incl, g_c - g_r, 0.0)), 0.0)

    kb = k.astype(bf16)
    kq_k = _bmm_nt(jnp.concatenate([kb, q.astype(bf16)], axis=1), kb)
    lmat = jnp.where(strict, beta_c * decay * kq_k[:, :n], 0.0)
    eg = jnp.exp(g_c)
    rhs = jnp.concatenate([beta_c * v, (beta_c * eg) * k], axis=-1)
    half = jnp.where((r // 2) == (cc // 2), lmat, 0.0)
    t = jnp.where(eye, 1.0, 0.0) - half
    kb_ = 2
    while kb_ < c:
        ck = jnp.where(((r // (2 * kb_)) == (cc // (2 * kb_))) & ((r // kb_) != (cc // kb_)), lmat, 0.0)
        t = t - _bmm(_bmm(t, ck), t)
        kb_ *= 2
    sol = _bmm(t, rhs)
    u0, w = sol[:, :, :DH_B], sol[:, :, DH_B:]

    aqk = (kq_k[:, n:] * decay).astype(bf16)
    qg = q * eg
    kg = k * jnp.exp(gtot_c - g_c)
    wide = (nb, n, H_B * DH_B)
    blk = (lax.broadcasted_iota(jnp.int32, wide, 1) // c) == (lax.broadcasted_iota(jnp.int32, wide, 2) // DH_B)
    spread = lambda m: jnp.where(blk, jnp.concatenate([m] * H_B, axis=2), 0.0).astype(bf16)
    tall = (nb, H_B * DH_B, n)
    blk_t = (lax.broadcasted_iota(jnp.int32, tall, 1) // DH_B) == (lax.broadcasted_iota(jnp.int32, tall, 2) // c)
    kg_t = jnp.stack([kg[i].T for i in range(nb)])
    kgt = jnp.where(blk_t, jnp.concatenate([kg_t] * H_B, axis=1), 0.0).astype(bf16)
    gl = jnp.exp(jnp.sum(jnp.where(blk_t, la_r, 0.0), axis=2, keepdims=True))
    return u0, jnp.concatenate([spread(w), spread(qg)], axis=1), jnp.concatenate([aqk, kgt], axis=1), gl


def _gdn_chunk_step(prep, idx, s_st):
    pick = lambda t: jnp.stack([t[i] for i in idx])
    u0, wq_bd, ak, gl = (pick(t) for t in prep)
    n = H_B * GDN_CHUNK
    ws_qs = _bmm(wq_bd, s_st)
    ub = (u0 - ws_qs[:, :n]).astype(bf16)
    au_ku = _bmm(ak, ub)
    o = ws_qs[:, n:] + au_ku[:, :n]
    return o, gl * s_st + au_ku[:, n:]


def _gdn_kernel(xf_ref, xb_ref, grf_ref, grb_ref, prm_ref, s0_ref, of_ref, ob_ref, s_ref, *, chunks):
    @pl.when(pl.program_id(1) == 0)
    def _():
        s_ref[...] = s0_ref[...]

    c = GDN_CHUNK
    rows = lambda j: slice(j * c, (j + 1) * c)
    x = jnp.stack([xf_ref[0, rows(j)] for j in range(chunks)] + [xb_ref[0, rows(j)] for j in range(chunks)])
    graw = jnp.stack([grf_ref[0, j, 0:2] for j in range(chunks)] + [grb_ref[0, j, 2:4] for j in range(chunks)])
    prm = jnp.stack([prm_ref[0:2]] * chunks + [prm_ref[2:4]] * chunks)
    prep = _gdn_chunk_prep(x, graw, prm, chunks)
    s = s_ref[0]
    heads_to_lanes = lambda o: jnp.concatenate([o[h * c:(h + 1) * c] for h in range(H_B)], axis=1)
    for j in range(chunks):
        jb = chunks - 1 - j
        o, s = _gdn_chunk_step(prep, (j, chunks + jb), s)
        of_ref[0, rows(j)] = heads_to_lanes(o[0]).astype(of_ref.dtype)
        ob_ref[0, rows(jb)] = heads_to_lanes(o[1]).astype(ob_ref.dtype)
    s_ref[0] = s


def gdn_scan(x, gates_row, prm, s0, chunks=2):
    b, l, _ = x.shape
    c = GDN_CHUNK
    assert l % (c * chunks) == 0
    nb = l // (c * chunks)
    rows = c * chunks
    fwd = lambda bi, i: (bi, i, 0)
    bwd = lambda bi, i: (bi, nb - 1 - i, 0)
    st_spec = pl.BlockSpec((1, 2, H_B * DH_B, DH_B), lambda bi, i: (bi, 0, 0, 0))
    g_blk = (1, chunks, 4, H_B * c)
    return pl.pallas_call(
        functools.partial(_gdn_kernel, chunks=chunks),
        out_shape=[jax.ShapeDtypeStruct((b, l, B_W), bf16), jax.ShapeDtypeStruct((b, l, B_W), bf16),
                   jax.ShapeDtypeStruct(s0.shape, f32)],
        grid=(b, nb),
        in_specs=[
            pl.BlockSpec((1, rows, 3 * B_W), fwd), pl.BlockSpec((1, rows, 3 * B_W), bwd),
            pl.BlockSpec(g_blk, lambda bi, i: (bi, i, 0, 0)),
            pl.BlockSpec(g_blk, lambda bi, i: (bi, nb - 1 - i, 0, 0)),
            pl.BlockSpec((4, H_B * c), lambda bi, i: (0, 0)),
            st_spec,
        ],
        out_specs=[pl.BlockSpec((1, rows, B_W), fwd), pl.BlockSpec((1, rows, B_W), bwd), st_spec],
        compiler_params=_cparams(("parallel", "arbitrary")),
        name="gdn_scan",
    )(x, x, gates_row, gates_row, prm, s0)


def _na_kernel(sc_ref, q_ref, k_ref, v_ref, kc_ref, vc_ref, bias_ref, o_ref, *, rb, band, rows, hp, bounded):
    i = pl.program_id(2)
    ub = jnp.clip(i * rb - WIN_R // 2, 0, rows - band)
    off = pl.multiple_of(ub * GRID_W, GRID_W)
    heads = lambda t: jnp.stack([t[:, h * DH_C:(h + 1) * DH_C] for h in range(hp)])
    q = heads(q_ref[0])
    kb = heads(k_ref[0, pl.ds(off, band * GRID_W), :])
    vb = heads(v_ref[0, pl.ds(off, band * GRID_W), :])
    kc = heads(kc_ref[0])
    vc = heads(vc_ref[0])
    bqk = lambda a, b_: jnp.einsum('hqd,hkd->hqk', a, b_, preferred_element_type=f32)
    bpv = lambda a, b_: jnp.einsum('hqk,hkd->hqd', a, b_, preferred_element_type=f32)
    s_lat = bqk(q, kb) + bias_ref[0]
    s_ctx = bqk(q, kc)
    if bounded:
        def with_ones(t):
            ones_col = jnp.where(lax.broadcasted_iota(jnp.int32, t.shape, 2) == 0, 1.0, 0.0).astype(bf16)
            return jnp.concatenate([t, ones_col], axis=2)

        bound = sc_ref[0]
        acc = (bpv(jnp.exp2(s_lat - bound).astype(bf16), with_ones(vb))
               + bpv(jnp.exp2(s_ctx - bound).astype(bf16), with_ones(vc)))
        o = acc[:, :, :DH_C] * (1.0 / jnp.sum(acc[:, :, DH_C:], axis=-1, keepdims=True))
    else:
        m = jnp.maximum(jnp.max(s_lat, axis=-1, keepdims=True), jnp.max(s_ctx, axis=-1, keepdims=True))
        p_lat = jnp.exp2(s_lat - m)
        p_ctx = jnp.exp2(s_ctx - m)
        den = jnp.sum(p_lat, axis=-1, keepdims=True) + jnp.sum(p_ctx, axis=-1, keepdims=True)
        o = (bpv(p_lat.astype(bf16), vb) + bpv(p_ctx.astype(bf16), vc)) * (1.0 / den)
    o_ref[0] = jnp.concatenate([o[h] for h in range(hp)], axis=-1).astype(o_ref.dtype)


def _na_bias_index(rows, rb, band):
    wr = min(WIN_R, rows)
    nblk = rows // rb
    cols = np.arange(GRID_W)
    c_start = np.clip(cols - WIN_C // 2, 0, GRID_W - WIN_C)
    col_ok = (cols[None, :] >= c_start[:, None]) & (cols[None, :] < c_start[:, None] + WIN_C)
    dc_idx = np.clip(cols[None, :] - cols[:, None] + WIN_C - 1, 0, 2 * WIN_C - 2)

    def geometry(i):
        ub = int(np.clip(i * rb - WIN_R // 2, 0, rows - band))
        qr = i * rb + np.arange(rb)
        kr = ub + np.arange(band)
        r_start = np.clip(qr - wr // 2, 0, rows - wr)
        row_ok = (kr[None, :] >= r_start[:, None]) & (kr[None, :] < r_start[:, None] + wr)
        dr = np.clip(kr[None, :] - qr[:, None] + WIN_R - 1, 0, 2 * WIN_R - 2)
        ok = row_ok[:, None, :, None] & col_ok[None, :, None, :]
        return ok.reshape(rb * GRID_W, band * GRID_W), dr

    reps = [0, min(1, nblk - 1), nblk - 1]
    geo = [geometry(i) for i in reps]
    for i in range(1, nblk - 1):
        g = geometry(i)
        assert all(np.array_equal(a, b_) for a, b_ in zip(g, geo[1]))
    ok = np.stack([g[0] for g in geo])
    dr = np.stack([g[1] for g in geo])
    dr_onehot = (dr[..., None] == np.arange(2 * WIN_R - 1)).astype(np.float32)
    dc_onehot = (dc_idx[None] == np.arange(2 * WIN_C - 1)[:, None, None]).astype(np.float32)
    return ok, dr_onehot, dc_onehot


def _na_bias_table(rpb, rows, rb, band):
    ok, dr_onehot, dc_onehot = _na_bias_index(rows, rb, band)
    hi = lax.Precision.HIGHEST
    by_col = jnp.einsum('hrc,cqk->hrqk', rpb.astype(f32), dc_onehot, precision=hi)
    slabs = jnp.einsum('vabr,hrqk->vhaqbk', dr_onehot, by_col, precision=hi)
    h = rpb.shape[0]
    return jnp.where(ok[:, None], slabs.reshape(3, h, rb * GRID_W, band * GRID_W) * math.log2(math.e), NEG)


def neighbourhood_attention(q, k, v, k_ctx, v_ctx, rpb, bound, bounded, rb=4):
    b, l, _ = q.shape
    lc = k_ctx.shape[1]
    rows = l // GRID_W
    band = rb + WIN_R - 1
    assert rows % rb == 0 and rows >= band and WIN_R <= rows
    nblk = rows // rb
    bias = _na_bias_table(rpb, rows, rb, band)
    tq = rb * GRID_W
    variant = lambda i: jnp.where(i == 0, 0, jnp.where(i == nblk - 1, 2, 1))
    hp = 4
    assert H_C % hp == 0
    hw = hp * DH_C
    head_blk = lambda n: pl.BlockSpec((1, n, hw), lambda bi, h, i: (bi, 0, h))
    return pl.pallas_call(
        functools.partial(_na_kernel, rb=rb, band=band, rows=rows, hp=hp, bounded=bounded),
        out_shape=jax.ShapeDtypeStruct((b, l, H_C * DH_C), bf16),
        grid=(b, H_C // hp, nblk),
        in_specs=[
            pl.BlockSpec(memory_space=pltpu.SMEM),
            pl.BlockSpec((1, tq, hw), lambda bi, h, i: (bi, i, h)),
            head_blk(l), head_blk(l), head_blk(lc), head_blk(lc),
            pl.BlockSpec((1, hp, tq, band * GRID_W), lambda bi, h, i: (variant(i), h, 0, 0)),
        ],
        out_specs=pl.BlockSpec((1, tq, hw), lambda bi, h, i: (bi, i, h)),
        compiler_params=_cparams(("parallel", "parallel", "arbitrary")),
        name="na_attn" if bounded else "na_attn_online",
    )(bound, q, k, v, k_ctx, v_ctx, bias)


def _outproj_kernel(*refs, has_gdn):
    if has_gdn:
        x_ref, g_ref, a_ref, wa_ref, of_ref, ob_ref, gate_ref, gain_ref, wb_ref, o_ref = refs
    else:
        x_ref, g_ref, a_ref, wa_ref, o_ref = refs
    acc = _dot(a_ref[0], wa_ref[...])
    if has_gdn:
        o = of_ref[0].astype(f32) + ob_ref[0].astype(f32)
        gate = gate_ref[0].astype(f32)
        parts = []
        for h in range(H_B):
            sl = slice(h * DH_B, (h + 1) * DH_B)
            t = o[:, sl]
            y = t * lax.rsqrt(jnp.mean(t * t, axis=-1, keepdims=True) + EPS) * gain_ref[...]
            parts.append((y * _silu(gate[:, sl])).astype(bf16))
        acc = acc + _dot(jnp.concatenate(parts, axis=-1), wb_ref[...])
    o_ref[0] = x_ref[0] + g_ref[0] * acc


def out_project(x, g, a, wa, gdn=None, tm=512, name="outproj"):
    b, l, d = x.shape
    tm = min(tm, l)
    assert l % tm == 0
    bm = g.shape[0]
    mod_map = (lambda bi, i: (bi, 0, 0)) if bm == b else (lambda bi, i: (0, 0, 0))
    row = lambda n: pl.BlockSpec((1, tm, n), lambda bi, i: (bi, i, 0))
    full = lambda arr: pl.BlockSpec(arr.shape, lambda bi, i: (0, 0))
    in_specs = [row(d), pl.BlockSpec((1, 1, d), mod_map), row(a.shape[2]), full(wa)]
    args = [x, g, a, wa]
    if gdn is not None:
        o_f, o_b, gate, gain, wb = gdn
        in_specs += [row(B_W), row(B_W), row(B_W), full(gain), full(wb)]
        args += [o_f, o_b, gate, gain, wb]
    return pl.pallas_call(
        functools.partial(_outproj_kernel, has_gdn=gdn is not None),
        out_shape=jax.ShapeDtypeStruct((b, l, d), f32),
        grid=(b, l // tm),
        in_specs=in_specs,
        out_specs=row(d),
        compiler_params=_cparams(("parallel", "parallel")),
        name=name,
    )(*args)


def _ffn_kernel(x_ref, gain_ref, sc_ref, sh_ref, g_ref, win_ref, wo_ref, o_ref, h_ref, acc_ref, *, tf):
    h_ref[...] = _norm_mod(x_ref[0], gain_ref[...], sc_ref[0], sh_ref[0]).astype(bf16)
    acc_ref[...] = jnp.zeros_like(acc_ref)
    ff = wo_ref.shape[0]
    tm = h_ref.shape[0]
    halves = [slice(r * (tm // 2), (r + 1) * (tm // 2)) for r in range(2)] if tm >= 512 else [slice(0, tm)]

    def body(j, carry):
        c0 = pl.multiple_of(j * tf, tf)
        wg = win_ref[:, pl.ds(c0, tf)]
        wu = win_ref[:, pl.ds(ff + c0, tf)]
        wo = wo_ref[pl.ds(c0, tf), :]
        gate_up = []
        for rows in halves:
            h = h_ref[rows, :]
            gate_up.append((_dot(h, wg), _dot(h, wu)))
        for rows, (gt, up) in zip(halves, gate_up):
            acc_ref[rows, :] += _dot((_silu(gt) * up).astype(bf16), wo)
        return carry

    lax.fori_loop(0, ff // tf, body, 0)
    o_ref[0] = x_ref[0] + g_ref[0] * acc_ref[...]


def ffn(x, gain, sc, sh, g, w_in, w_out, tm=1024, tf=256, name="ffn"):
    b, l, d = x.shape
    ff = w_out.shape[0]
    tm = min(tm, l)
    assert l % tm == 0 and ff % tf == 0
    bm = sc.shape[0]
    mod_map = (lambda bi, i: (bi, 0, 0)) if bm == b else (lambda bi, i: (0, 0, 0))
    mod = pl.BlockSpec((1, 1, d), mod_map)
    resident = lambda arr: pl.BlockSpec(arr.shape, lambda bi, i: (0, 0), pipeline_mode=pl.Buffered(1))
    return pl.pallas_call(
        functools.partial(_ffn_kernel, tf=tf),
        out_shape=jax.ShapeDtypeStruct((b, l, d), f32),
        grid=(b, l // tm),
        in_specs=[
            pl.BlockSpec((1, tm, d), lambda bi, i: (bi, i, 0)),
            pl.BlockSpec((1, d), lambda bi, i: (0, 0)),
            mod, mod, mod,
            resident(w_in), resident(w_out),
        ],
        out_specs=pl.BlockSpec((1, tm, d), lambda bi, i: (bi, i, 0)),
        scratch_shapes=[pltpu.VMEM((tm, d), bf16), pltpu.VMEM((tm, d), f32)],
        compiler_params=_cparams(("parallel", "parallel")),
        name=name,
    )(x, gain, sc, sh, g, w_in, w_out)


def _rope_tables(l):
    n_freq = DH_A // 4
    inv = ROPE_BASE ** (-jnp.arange(n_freq, dtype=f32) / n_freq)
    t = jnp.arange(l)
    row = (t // GRID_W).astype(f32)
    col = (t % GRID_W).astype(f32)
    ang = jnp.concatenate([row[:, None] * inv, col[:, None] * inv], axis=-1)
    cos = jnp.repeat(jnp.cos(ang), 2, axis=-1)
    sin = jnp.repeat(jnp.sin(ang), 2, axis=-1)
    sign = jnp.tile(jnp.array([-1.0, 1.0], f32), DH_A // 2)
    return jnp.tile(cos, (1, 2)), jnp.tile(sin * sign, (1, 2))


def _pad_rows(a, rows):
    return jnp.zeros((rows,) + a.shape[1:], a.dtype).at[:a.shape[0]].set(a)


def kernel(x, c, ctx, c_ctx, ada_w, ada_b, norm_mix, norm_ffn, ffn_w_in, ffn_w_out, even_w_in, even_w_out,
           diff_qk_gain, diff_lambda, diff_subln, gdn_conv, gdn_a_log, gdn_dt_bias, gdn_norm, odd_w_in,
           odd_w_out, na_qk_gain, na_rpb):
    b, l, d = x.shape
    lc = ctx.shape[1]
    depth = ada_w.shape[0]
    c_all = _pad_rows(jnp.concatenate([c, c_ctx[None]], axis=0), 16)
    mods = ada_modulation(c_all, ada_w, ada_b)
    cos_t, sin_t = _rope_tables(l)
    ctx_flat = ctx.reshape(1, b * lc, d)

    for layer in range(depth):
        ctx_out = layer < depth - 1
        m_lat = [t[:, None, :] for t in jnp.split(mods[layer, :b], 6, axis=-1)]
        m_ctx = [t[:, None, :] for t in jnp.split(mods[layer, b:b + 1], 6, axis=-1)]
        sh_m, sc_m, g_m, sh_f, sc_f, g_f = m_lat
        csh_m, csc_m, cg_m, csh_f, csc_f, cg_f = m_ctx
        gain_m = norm_mix[layer][None]
        gain_f = norm_ffn[layer][None]
        if layer % 2 == 0:
            e = layer // 2
            lam_init = 0.8 - 0.6 * math.exp(-0.3 * layer)
            w_in = even_w_in[e].astype(bf16)
            cuts = [0, A_W, 2 * A_W, 3 * A_W, 3 * A_W + 3 * B_W, 3 * A_W + 4 * B_W]
            ws = [w_in[:, cuts[i]:cuts[i + 1]] for i in range(5)]
            w_gates = jnp.zeros((d, LANES), bf16).at[:, :4 * H_B].set(w_in[:, cuts[5]:])
            ws.append(w_gates)
            qk_gain = _pad_rows(jnp.tile(diff_qk_gain[e], (1, 2)), 8)
            dts = [bf16, bf16, bf16, f32, bf16, f32]
            q_scale = DH_A ** -0.5 * math.log2(math.e)
            epi = lambda rope: [("submap", 0, rope, q_scale), ("submap", 1, rope, 1.0), ("plain",),
                                ("plain",), ("plain",), ("plain",)]
            qa, ka, va, qkv_b, g_b, gates = norm_mod_project(
                x, gain_m, sc_m, sh_m, ws, dts, epi(True), qk_gain, rope_tabs=(cos_t, sin_t), name="even_proj")
            qac, kac, vac, qkv_bc, g_bc, gates_c = norm_mod_project(
                ctx_flat, gain_m, csc_m, csh_m, ws, dts, epi(False), qk_gain, name="even_proj_ctx")
            unflat = lambda t: t.reshape(b, lc, t.shape[-1])
            qac, kac, vac, qkv_bc, g_bc, gates_c = map(unflat, (qac, kac, vac, qkv_bc, g_bc, gates_c))

            lv = diff_lambda[e]
            lam = jnp.exp(jnp.sum(lv[0] * lv[1])) - jnp.exp(jnp.sum(lv[2] * lv[3])) + lam_init
            gmax = jnp.max(jnp.abs(diff_qk_gain[e]), axis=-1)
            bound = 1.02 * DH_A * q_scale * gmax[0] * gmax[1]
            scal = jnp.stack([lam, bound]).astype(f32)
            subln = diff_subln[e][None]
            attn_args = (scal, qa, kac, vac, ka, va, subln, lam_init)
            a_lat = lax.cond(2.0 * bound < 120.0,
                             lambda: diff_attention_bounded(*attn_args),
                             lambda: diff_attention(*attn_args, name="diff_attn_online"))
            a_ctx = diff_attention(scal, qac, kac, vac, None, None, subln, lam_init, name="diff_attn_ctx")

            y_lat = gdn_short_conv(qkv_b, gdn_conv[e])
            y_ctx = gdn_short_conv(qkv_bc, gdn_conv[e])
            per_pos = lambda p: jnp.repeat(p, GDN_CHUNK)
            prm = jnp.stack([per_pos(gdn_a_log[e][0]), per_pos(gdn_dt_bias[e][0]),
                             per_pos(gdn_a_log[e][1]), per_pos(gdn_dt_bias[e][1])])

            def to_rows(g):
                t = g[:, :, :4 * H_B].reshape(g.shape[0], -1, GDN_CHUNK, 4, H_B).transpose(0, 1, 3, 4, 2)
                t = t.reshape(g.shape[0], -1, 4, H_B * GDN_CHUNK)
                return t[:, :, jnp.array([0, 2, 1, 3])]

            s0 = jnp.zeros((b, 2, H_B * DH_B, DH_B), f32)
            ocf, ocb, s_mid = gdn_scan(y_ctx, to_rows(gates_c), prm, s0)
            olf, olb, _ = gdn_scan(y_lat, to_rows(gates), prm, s_mid)

            w_out = even_w_out[e].astype(bf16)
            gdn_gain = gdn_norm[e][None]
            x = out_project(x, g_m, a_lat, w_out[:A_W], gdn=(olf, olb, g_b, gdn_gain, w_out[A_W:]),
                            name="even_out")
            ctx_flat = out_project(
                ctx_flat, cg_m, a_ctx.reshape(1, b * lc, A_W), w_out[:A_W],
                gdn=(ocf.reshape(1, b * lc, B_W), ocb.reshape(1, b * lc, B_W), g_bc.reshape(1, b * lc, B_W),
                     gdn_gain, w_out[A_W:]), name="even_out_ctx")
        else:
            od = layer // 2
            w_in = odd_w_in[od].astype(bf16)
            mix = H_C * DH_C
            ws = [w_in[:, :mix], w_in[:, mix:2 * mix], w_in[:, 2 * mix:]]
            qk_gain = _pad_rows(na_qk_gain[od], 8)
            q_scale = DH_C ** -0.5 * math.log2(math.e)
            q, k, v = norm_mod_project(
                x, gain_m, sc_m, sh_m, ws, [bf16] * 3,
                [("headnorm", 0, q_scale), ("headnorm", 1, 1.0), ("plain",)], qk_gain, name="odd_proj")
            kc, vc = norm_mod_project(
                ctx_flat, gain_m, csc_m, csh_m, ws[1:], [bf16] * 2,
                [("headnorm", 1, 1.0), ("plain",)], qk_gain, name="odd_proj_ctx")
            kc, vc = kc.reshape(b, lc, mix), vc.reshape(b, lc, mix)
            gmax = jnp.max(jnp.abs(na_qk_gain[od]), axis=-1)
            bound = (1.02 * DH_C * q_scale * gmax[0] * gmax[1]
                     + jnp.max(jnp.abs(na_rpb[od])) * math.log2(math.e)).astype(f32).reshape(1)
            na = functools.partial(neighbourhood_attention, q, k, v, kc, vc, na_rpb[od], bound)
            o = lax.cond(2.0 * bound[0] < 120.0, lambda: na(True), lambda: na(False))
            x = out_project(x, g_m, o, odd_w_out[od].astype(bf16), name="odd_out")
            if ctx_out:
                raise NotImplementedError("context output of a neighbourhood layer is not needed at depth 2")

        wf_in = ffn_w_in[layer].astype(bf16)
        wf_out = ffn_w_out[layer].astype(bf16)
        x = ffn(x, gain_f, sc_f, sh_f, g_f, wf_in, wf_out, name="ffn")
        if ctx_out:
            ctx_flat = ffn(ctx_flat, gain_f, csc_f, csh_f, cg_f, wf_in, wf_out, name="ffn_ctx")
    return x
```

```python
import functools
import math

import numpy as np
import jax
import jax.numpy as jnp
from jax import lax
from jax.experimental import pallas as pl
from jax.experimental.pallas import tpu as pltpu

f32 = jnp.float32
bf16 = jnp.bfloat16

EPS = 1e-6
ROPE_BASE = 10000.0
GRID_W = 64
H_A = 4
DH_A = 64
DV_A = 2 * DH_A
A_W = H_A * DV_A
H_B = 4
DH_B = 128
B_W = H_B * DH_B
GDN_CHUNK = 64
CONV_K = 5
H_C = 8
DH_C = 128
WIN_R = 8
WIN_C = 16

LANES = 128
VMEM_LIMIT = 56 * 1024 * 1024
NEG = -1e30


def _cparams(sem):
    return pltpu.CompilerParams(dimension_semantics=sem, vmem_limit_bytes=VMEM_LIMIT)


def _silu(x):
    return x * (1.0 / (1.0 + jnp.exp(-x)))


def _dot(a, b):
    return jnp.dot(a, b, preferred_element_type=f32)


def _dot_nt(a, b):
    return lax.dot_general(a, b, (((1,), (1,)), ((), ())), preferred_element_type=f32)


def _split_bf16(x):
    hi = x.astype(bf16)
    lo = (x - hi.astype(f32)).astype(bf16)
    return hi, lo


def _dot3(a, b):
    ah, al = _split_bf16(a)
    bh, bl = _split_bf16(b)
    return _dot(ah, bh) + (_dot(ah, bl) + _dot(al, bh))


def _mm(a, b):
    return _dot(a.astype(bf16), b.astype(bf16))


def _dot3_nt(a, b):
    ah, al = _split_bf16(a)
    bh, bl = _split_bf16(b)
    return _dot_nt(ah, bh) + (_dot_nt(ah, bl) + _dot_nt(al, bh))


def _ada_kernel(c_ref, w_ref, b_ref, o_ref):
    sc = _silu(c_ref[...])
    o_ref[0] = _dot3(sc, w_ref[0]) + b_ref[0]


def ada_modulation(c_all, ada_w, ada_b):
    depth, d, n = ada_w.shape
    rows = c_all.shape[0]
    tn = 1536
    return pl.pallas_call(
        _ada_kernel,
        out_shape=jax.ShapeDtypeStruct((depth, rows, n), f32),
        grid=(depth, n // tn),
        in_specs=[
            pl.BlockSpec((rows, d), lambda l, j: (0, 0)),
            pl.BlockSpec((1, d, tn), lambda l, j: (l, 0, j)),
            pl.BlockSpec((1, 1, tn), lambda l, j: (l, 0, j)),
        ],
        out_specs=pl.BlockSpec((1, rows, tn), lambda l, j: (l, 0, j)),
        compiler_params=_cparams(("parallel", "parallel")),
        name="ada_modulation",
    )(c_all, ada_w, ada_b.reshape(depth, 1, n))


def _norm_mod(x, gain, sc, sh):
    y = x * lax.rsqrt(jnp.mean(x * x, axis=-1, keepdims=True) + EPS)
    return (y * gain) * (1.0 + sc) + sh


def _swap_pairs(x):
    lane = lax.broadcasted_iota(jnp.int32, x.shape, x.ndim - 1)
    nxt = pltpu.roll(x, x.shape[-1] - 1, x.ndim - 1)
    prv = pltpu.roll(x, 1, x.ndim - 1)
    return jnp.where(lane % 2 == 0, nxt, prv)


def _submap_norm_rope(t, gain, cos, sin, scale):
    lane = lax.broadcasted_iota(jnp.int32, t.shape, 1)
    lo = lane < DH_A
    sq = t * t
    s_lo = jnp.sum(jnp.where(lo, sq, 0.0), axis=-1, keepdims=True)
    s_hi = jnp.sum(jnp.where(lo, 0.0, sq), axis=-1, keepdims=True)
    r = jnp.where(lo, lax.rsqrt(s_lo * (1.0 / DH_A) + EPS), lax.rsqrt(s_hi * (1.0 / DH_A) + EPS))
    y = t * r * gain
    if cos is not None:
        y = y * cos + _swap_pairs(y) * sin
    if scale != 1.0:
        y = y * scale
    return y


def _head_norm(t, gain, scale):
    y = t * lax.rsqrt(jnp.mean(t * t, axis=-1, keepdims=True) + EPS) * gain
    if scale != 1.0:
        y = y * scale
    return y


def _proj_kernel(*refs, epilogues, rope):
    n_out = len(epilogues)
    x_ref, gain_ref, sc_ref, sh_ref = refs[:4]
    pos = 4
    if rope:
        cos_ref, sin_ref = refs[pos:pos + 2]
        pos += 2
    qkg_ref = refs[pos]
    pos += 1
    w_refs = refs[pos:pos + n_out]
    o_refs = refs[pos + n_out:pos + 2 * n_out]

    h = _norm_mod(x_ref[0], gain_ref[...], sc_ref[0], sh_ref[0]).astype(bf16)
    for w_ref, o_ref, epi in zip(w_refs, o_refs, epilogues):
        acc = _dot(h, w_ref[...])
        kind = epi[0]
        if kind == "plain":
            o_ref[0] = acc.astype(o_ref.dtype)
        elif kind == "submap":
            _, grow, use_rope, scale = epi
            gain = qkg_ref[grow:grow + 1, :]
            for hd in range(acc.shape[1] // LANES):
                t = acc[:, hd * LANES:(hd + 1) * LANES]
                cs = (cos_ref[...], sin_ref[...]) if (rope and use_rope) else (None, None)
                y = _submap_norm_rope(t, gain, cs[0], cs[1], scale)
                o_ref[0, :, hd * LANES:(hd + 1) * LANES] = y.astype(o_ref.dtype)
        elif kind == "headnorm":
            _, grow, scale = epi
            gain = qkg_ref[grow:grow + 1, :]
            for hd in range(acc.shape[1] // LANES):
                t = acc[:, hd * LANES:(hd + 1) * LANES]
                o_ref[0, :, hd * LANES:(hd + 1) * LANES] = _head_norm(t, gain, scale).astype(o_ref.dtype)
        else:
            raise ValueError(kind)


def norm_mod_project(x, gain, sc, sh, weights, out_dtypes, epilogues, qk_gain, rope_tabs=None,
                     tm=512, name="proj"):
    b, l, d = x.shape
    tm = min(tm, l)
    assert l % tm == 0
    bm = sc.shape[0]
    mod_map = (lambda bi, i: (bi, 0, 0)) if bm == b else (lambda bi, i: (0, 0, 0))
    rope = rope_tabs is not None
    in_specs = [
        pl.BlockSpec((1, tm, d), lambda bi, i: (bi, i, 0)),
        pl.BlockSpec((1, d), lambda bi, i: (0, 0)),
        pl.BlockSpec((1, 1, d), mod_map),
        pl.BlockSpec((1, 1, d), mod_map),
    ]
    args = [x, gain, sc, sh]
    if rope:
        in_specs += [pl.BlockSpec((tm, LANES), lambda bi, i: (i, 0))] * 2
        args += list(rope_tabs)
    in_specs.append(pl.BlockSpec(qk_gain.shape, lambda bi, i: (0, 0)))
    args.append(qk_gain)
    for w in weights:
        in_specs.append(pl.BlockSpec(w.shape, lambda bi, i: (0, 0)))
        args.append(w)
    out_shape = [jax.ShapeDtypeStruct((b, l, w.shape[1]), dt) for w, dt in zip(weights, out_dtypes)]
    out_specs = [pl.BlockSpec((1, tm, w.shape[1]), lambda bi, i: (bi, i, 0)) for w in weights]
    return pl.pallas_call(
        functools.partial(_proj_kernel, epilogues=tuple(epilogues), rope=rope),
        out_shape=out_shape,
        grid=(b, l // tm),
        in_specs=in_specs,
        out_specs=out_specs,
        compiler_params=_cparams(("parallel", "parallel")),
        name=name,
    )(*args)


def _diff_attn_kernel(*refs, has_lat, tk, lam_init):
    if has_lat:
        sc_ref, q_ref, kc_ref, vc_ref, k_ref, v_ref, subln_ref, o_ref = refs
    else:
        sc_ref, q_ref, kc_ref, vc_ref, subln_ref, o_ref = refs
    q = q_ref[0]
    tq = q.shape[0]
    lane = lax.broadcasted_iota(jnp.int32, q.shape, 1)
    zero = jnp.zeros_like(q)
    q1 = jnp.where(lane < DH_A, q, zero)
    q2 = jnp.where(lane < DH_A, zero, q)

    def kv_loop(step, carry):
        carry = step(kc_ref[0], vc_ref[0], carry)
        if has_lat:
            def body(j, c):
                off = pl.multiple_of(j * tk, tk)
                return step(k_ref[0, pl.ds(off, tk), :], v_ref[0, pl.ds(off, tk), :], c)

            carry = lax.fori_loop(0, k_ref.shape[1] // tk, body, carry)
        return carry

    def one_map(qm, kblk, vblk, m, l, acc):
        s = _dot_nt(qm, kblk)
        m_new = jnp.maximum(m, jnp.max(s, axis=-1, keepdims=True))
        p = jnp.exp2(s - m_new)
        a = jnp.exp2(m - m_new)
        l = a * l + jnp.sum(p, axis=-1, keepdims=True)
        acc = a * acc + _dot(p.astype(bf16), vblk)
        return m_new, l, acc

    def step(kblk, vblk, carry):
        m1, l1, a1, m2, l2, a2 = carry
        m1, l1, a1 = one_map(q1, kblk, vblk, m1, l1, a1)
        m2, l2, a2 = one_map(q2, kblk, vblk, m2, l2, a2)
        return m1, l1, a1, m2, l2, a2

    mi = jnp.full((tq, 1), NEG, f32)
    li = jnp.zeros((tq, 1), f32)
    ai = jnp.zeros((tq, DV_A), f32)
    _, l1, a1, _, l2, a2 = kv_loop(step, (mi, li, ai, mi, li, ai))
    lam = sc_ref[0]
    o = a1 * (1.0 / l1) - lam * (a2 * (1.0 / l2))
    y = o * lax.rsqrt(jnp.mean(o * o, axis=-1, keepdims=True) + EPS) * subln_ref[...]
    o_ref[0] = (y * (1.0 - lam_init)).astype(o_ref.dtype)


def diff_attention(lam, q, k_ctx, v_ctx, k_lat, v_lat, subln, lam_init, tq=512, tk=512, name="diff_attn"):
    b, lq, _ = q.shape
    lc = k_ctx.shape[1]
    has_lat = k_lat is not None
    tq = min(tq, lq)
    assert lq % tq == 0
    head_blk = lambda rows: pl.BlockSpec((1, rows, LANES), lambda bi, h, i: (bi, 0, h))
    in_specs = [
        pl.BlockSpec(memory_space=pltpu.SMEM),
        pl.BlockSpec((1, tq, LANES), lambda bi, h, i: (bi, i, h)),
        head_blk(lc), head_blk(lc),
    ]
    args = [lam, q, k_ctx, v_ctx]
    if has_lat:
        ll = k_lat.shape[1]
        tk = min(tk, ll)
        assert ll % tk == 0
        in_specs += [head_blk(ll), head_blk(ll)]
        args += [k_lat, v_lat]
    in_specs.append(pl.BlockSpec((1, LANES), lambda bi, h, i: (0, 0)))
    args.append(subln)
    return pl.pallas_call(
        functools.partial(_diff_attn_kernel, has_lat=has_lat, tk=tk, lam_init=lam_init),
        out_shape=jax.ShapeDtypeStruct((b, lq, A_W), bf16),
        grid=(b, H_A, lq // tq),
        in_specs=in_specs,
        out_specs=pl.BlockSpec((1, tq, LANES), lambda bi, h, i: (bi, i, h)),
        compiler_params=_cparams(("parallel", "parallel", "arbitrary")),
        name=name,
    )(*args)


def _diff_attn_t_kernel(sc_ref, qt_ref, kc_ref, vct_ref, k_ref, vt_ref, subln_ref, o_ref, *, tk, lam_init):
    qt = qt_ref[0, 0]
    tq = qt.shape[1]
    row = lax.broadcasted_iota(jnp.int32, qt.shape, 0)
    zero = jnp.zeros_like(qt)
    q12t = jnp.concatenate([jnp.where(row < DH_A, qt, zero), jnp.where(row < DH_A, zero, qt)], axis=1)
    bound = sc_ref[1]

    def step(kblk, vtblk, acc):
        return acc + _dot(vtblk, jnp.exp2(_dot(kblk, q12t) - bound).astype(bf16))

    acc = step(kc_ref[0], vct_ref[0, 0], jnp.zeros((vt_ref.shape[2], 2 * tq), f32))

    def body(j, a):
        off = pl.multiple_of(j * tk, tk)
        return step(k_ref[0, pl.ds(off, tk), :], vt_ref[0, 0, :, pl.ds(off, tk)], a)

    acc = lax.fori_loop(0, k_ref.shape[1] // tk, body, acc)
    inv = 1.0 / acc[DV_A:DV_A + 1, :]
    a = acc[:DV_A, :] * inv
    o = (a[:, :tq] - sc_ref[0] * a[:, tq:]).T
    y = o * lax.rsqrt(jnp.mean(o * o, axis=-1, keepdims=True) + EPS) * subln_ref[...]
    o_ref[0] = (y * (1.0 - lam_init)).astype(o_ref.dtype)


V_AUG_ROWS = DV_A + 16


def _heads_transposed(t, ones_row):
    b, l, _ = t.shape
    tt = t.reshape(b, l, H_A, DV_A).transpose(0, 2, 3, 1)
    if not ones_row:
        return tt
    extra = jnp.zeros((b, H_A, V_AUG_ROWS - DV_A, l), t.dtype).at[:, :, 0].set(1)
    return jnp.concatenate([tt, extra], axis=2)


def diff_attention_bounded(scal, q, k_ctx, v_ctx, k_lat, v_lat, subln, lam_init, tq=512, tk=4096,
                           name="diff_attn"):
    b, lq, _ = q.shape
    lc, ll = k_ctx.shape[1], k_lat.shape[1]
    tq, tk = min(tq, lq), min(tk, ll)
    assert lq % tq == 0 and ll % tk == 0
    qt = _heads_transposed(q, False)
    vct = _heads_transposed(v_ctx, True)
    vt = _heads_transposed(v_lat, True)
    return pl.pallas_call(
        functools.partial(_diff_attn_t_kernel, tk=tk, lam_init=lam_init),
        out_shape=jax.ShapeDtypeStruct((b, lq, A_W), bf16),
        grid=(b, H_A, lq // tq),
        in_specs=[
            pl.BlockSpec(memory_space=pltpu.SMEM),
            pl.BlockSpec((1, 1, DV_A, tq), lambda bi, h, i: (bi, h, 0, i)),
            pl.BlockSpec((1, lc, LANES), lambda bi, h, i: (bi, 0, h)),
            pl.BlockSpec((1, 1, V_AUG_ROWS, lc), lambda bi, h, i: (bi, h, 0, 0)),
            pl.BlockSpec((1, ll, LANES), lambda bi, h, i: (bi, 0, h)),
            pl.BlockSpec((1, 1, V_AUG_ROWS, ll), lambda bi, h, i: (bi, h, 0, 0)),
            pl.BlockSpec((1, LANES), lambda bi, h, i: (0, 0)),
        ],
        out_specs=pl.BlockSpec((1, tq, LANES), lambda bi, h, i: (bi, i, h)),
        compiler_params=_cparams(("parallel", "parallel", "arbitrary")),
        name=name,
    )(scal, qt, k_ctx, vct, k_lat, vt, subln)


def _gdn_conv_kernel(x_ref, w_ref, o_ref, pad_ref, *, rows_per_step):
    l = x_ref.shape[1]
    half = CONV_K // 2
    halo = 8
    zeros = jnp.zeros((halo, LANES), f32)
    pad_ref[0:halo, :] = zeros
    pad_ref[halo + l:halo + l + halo, :] = zeros
    pad_ref[halo:halo + l, :] = x_ref[0]
    kind = pl.program_id(1) // H_B
    w = w_ref[...]
    r = rows_per_step

    def body(i, _):
        t0 = pl.multiple_of(i * r, r)
        y = jnp.zeros((r, LANES), f32)
        for j in range(CONV_K):
            y = y + pad_ref[pl.ds(t0 + (halo - half + j), r), :] * w[j:j + 1, :]
        y = _silu(y)
        nrm = lax.rsqrt(jnp.sum(y * y, axis=-1, keepdims=True) + EPS)
        nrm = jnp.where(kind == 0, nrm * (DH_B ** -0.5), nrm)
        y = jnp.where(kind == 2, y, y * nrm)
        o_ref[0, pl.ds(t0, r), :] = y
        return 0

    lax.fori_loop(0, l // r, body, 0)


def gdn_short_conv(qkv, conv_w):
    b, l, c = qkv.shape
    r = min(512, l)
    assert l % r == 0
    wpad = jnp.zeros((8, c), f32).at[:CONV_K].set(conv_w)
    return pl.pallas_call(
        functools.partial(_gdn_conv_kernel, rows_per_step=r),
        out_shape=jax.ShapeDtypeStruct((b, l, c), f32),
        grid=(b, c // LANES),
        in_specs=[
            pl.BlockSpec((1, l, LANES), lambda bi, j: (bi, 0, j)),
            pl.BlockSpec((8, LANES), lambda bi, j: (0, j)),
        ],
        out_specs=pl.BlockSpec((1, l, LANES), lambda bi, j: (bi, 0, j)),
        scratch_shapes=[pltpu.VMEM((l + 16, LANES), f32)],
        compiler_params=_cparams(("parallel", "parallel")),
        name="gdn_conv",
    )(qkv, wpad)


def _softplus(x):
    return jnp.maximum(x, 0.0) + jnp.log1p(jnp.exp(-jnp.abs(x)))


def _sigmoid(x):
    return 1.0 / (1.0 + jnp.exp(-x))


def _stack_heads(x, base):
    return jnp.concatenate([x[:, base + h * DH_B:base + (h + 1) * DH_B] for h in range(H_B)], axis=0)


def _bmm(a, b):
    return jnp.einsum('bij,bjk->bik', a.astype(bf16), b.astype(bf16), preferred_element_type=f32)


def _bmm_nt(a, b):
    return jnp.einsum('bik,bjk->bij', a.astype(bf16), b.astype(bf16), preferred_element_type=f32)


def _gdn_chunk_prep(x, graw, prm, n_fwd):
    c = GDN_CHUNK
    n = H_B * c
    nb = x.shape[0]
    stack = lambda base: jnp.concatenate(
        [x[:, :, base + h * DH_B:base + (h + 1) * DH_B] for h in range(H_B)], axis=1)
    q, k, v = stack(0), stack(B_W), stack(2 * B_W)
    shp = (nb, n, n)
    chain = lax.broadcasted_iota(jnp.int32, shp, 0)
    r = lax.broadcasted_iota(jnp.int32, shp, 1)
    cc = lax.broadcasted_iota(jnp.int32, shp, 2)
    same = (r // c) == (cc // c)
    eye = r == cc
    ahead = jnp.where(chain < n_fwd, r - cc, cc - r)
    far = jnp.int32(4 * n)
    strict = jnp.where(same, ahead, -far) > 0
    incl = jnp.where(same, ahead, -far) >= 0
    incl_t = jnp.where(same, ahead, far) <= 0

    beta_r = _sigmoid(graw[:, 0:1, :])
    la_r = -jnp.exp(prm[:, 0:1, :]) * _softplus(graw[:, 1:2, :] + prm[:, 1:2, :])
    to_col = lambda row: jnp.sum(jnp.where(eye, row, 0.0), axis=2, keepdims=True)
    beta_c = to_col(beta_r)
    la_c = to_col(la_r)
    g_c = jnp.sum(jnp.where(incl, la_r, 0.0), axis=2, keepdims=True)
    g_r = jnp.sum(jnp.where(incl_t, la_c, 0.0), axis=1, keepdims=True)
    gtot_c = jnp.sum(jnp.where(same, la_r, 0.0), axis=2, keepdims=True)
    decay = jnp.where(incl, jnp.exp(jnp.where(incl, g_c - g_r, 0.0)), 0.0)

    kb = k.astype(bf16)
    kq_k = _bmm_nt(jnp.concatenate([kb, q.astype(bf16)], axis=1), kb)
    lmat = jnp.where(strict, beta_c * decay * kq_k[:, :n], 0.0)
    eg = jnp.exp(g_c)
    rhs = jnp.concatenate([beta_c * v, (beta_c * eg) * k], axis=-1)
    half = jnp.where((r // 2) == (cc // 2), lmat, 0.0)
    t = jnp.where(eye, 1.0, 0.0) - half
    kb_ = 2
    while kb_ < c:
        ck = jnp.where(((r // (2 * kb_)) == (cc // (2 * kb_))) & ((r // kb_) != (cc // kb_)), lmat, 0.0)
        t = t - _bmm(_bmm(t, ck), t)
        kb_ *= 2
    sol = _bmm(t, rhs)
    u0, w = sol[:, :, :DH_B], sol[:, :, DH_B:]

    aqk = (kq_k[:, n:] * decay).astype(bf16)
    qg = q * eg
    kg = k * jnp.exp(gtot_c - g_c)
    per_head = lambda m: m.astype(bf16).reshape(nb, H_B, c, DH_B)
    wq = jnp.concatenate([per_head(w), per_head(qg)], axis=2)
    tall = (nb, H_B * DH_B, n)
    blk_t = (lax.broadcasted_iota(jnp.int32, tall, 1) // DH_B) == (lax.broadcasted_iota(jnp.int32, tall, 2) // c)
    kg_t = jnp.stack([kg[i].T for i in range(nb)])
    kgt = jnp.where(blk_t, jnp.concatenate([kg_t] * H_B, axis=1), 0.0).astype(bf16)
    gl = jnp.exp(jnp.sum(jnp.where(blk_t, la_r, 0.0), axis=2, keepdims=True))
    return u0, wq, jnp.concatenate([aqk, kgt], axis=1), gl


def _gdn_chunk_step(prep, idx, s_st):
    pick = lambda t: jnp.stack([t[i] for i in idx])
    u0, wq, ak, gl = (pick(t) for t in prep)
    c = GDN_CHUNK
    n = H_B * c
    nd = s_st.shape[0]
    ws_qs = _bmm(wq.reshape(nd * H_B, 2 * c, DH_B), s_st.reshape(nd * H_B, DH_B, DH_B))
    ws_qs = ws_qs.reshape(nd, H_B, 2 * c, DH_B)
    ws = ws_qs[:, :, :c].reshape(nd, n, DH_B)
    qs = ws_qs[:, :, c:].reshape(nd, n, DH_B)
    ub = (u0 - ws).astype(bf16)
    au_ku = _bmm(ak, ub)
    o = qs + au_ku[:, :n]
    return o, gl * s_st + au_ku[:, n:]


def _gdn_kernel(xf_ref, xb_ref, grf_ref, grb_ref, prm_ref, s0_ref, of_ref, ob_ref, s_ref, *, chunks):
    @pl.when(pl.program_id(1) == 0)
    def _():
        s_ref[...] = s0_ref[...]

    c = GDN_CHUNK
    rows = lambda j: slice(j * c, (j + 1) * c)
    x = jnp.stack([xf_ref[0, rows(j)] for j in range(chunks)] + [xb_ref[0, rows(j)] for j in range(chunks)])
    graw = jnp.stack([grf_ref[0, j, 0:2] for j in range(chunks)] + [grb_ref[0, j, 2:4] for j in range(chunks)])
    prm = jnp.stack([prm_ref[0:2]] * chunks + [prm_ref[2:4]] * chunks)
    prep = _gdn_chunk_prep(x, graw, prm, chunks)
    s = s_ref[0]
    heads_to_lanes = lambda o: jnp.concatenate([o[h * c:(h + 1) * c] for h in range(H_B)], axis=1)
    for j in range(chunks):
        jb = chunks - 1 - j
        o, s = _gdn_chunk_step(prep, (j, chunks + jb), s)
        of_ref[0, rows(j)] = heads_to_lanes(o[0]).astype(of_ref.dtype)
        ob_ref[0, rows(jb)] = heads_to_lanes(o[1]).astype(ob_ref.dtype)
    s_ref[0] = s


def gdn_scan(x, gates_row, prm, s0, chunks=4):
    b, l, _ = x.shape
    c = GDN_CHUNK
    assert l % (c * chunks) == 0
    nb = l // (c * chunks)
    rows = c * chunks
    fwd = lambda bi, i: (bi, i, 0)
    bwd = lambda bi, i: (bi, nb - 1 - i, 0)
    st_spec = pl.BlockSpec((1, 2, H_B * DH_B, DH_B), lambda bi, i: (bi, 0, 0, 0))
    g_blk = (1, chunks, 4, H_B * c)
    return pl.pallas_call(
        functools.partial(_gdn_kernel, chunks=chunks),
        out_shape=[jax.ShapeDtypeStruct((b, l, B_W), bf16), jax.ShapeDtypeStruct((b, l, B_W), bf16),
                   jax.ShapeDtypeStruct(s0.shape, f32)],
        grid=(b, nb),
        in_specs=[
            pl.BlockSpec((1, rows, 3 * B_W), fwd), pl.BlockSpec((1, rows, 3 * B_W), bwd),
            pl.BlockSpec(g_blk, lambda bi, i: (bi, i, 0, 0)),
            pl.BlockSpec(g_blk, lambda bi, i: (bi, nb - 1 - i, 0, 0)),
            pl.BlockSpec((4, H_B * c), lambda bi, i: (0, 0)),
            st_spec,
        ],
        out_specs=[pl.BlockSpec((1, rows, B_W), fwd), pl.BlockSpec((1, rows, B_W), bwd), st_spec],
        compiler_params=_cparams(("parallel", "arbitrary")),
        name="gdn_scan",
    )(x, x, gates_row, gates_row, prm, s0)


def _na_kernel(sc_ref, q_ref, k_ref, v_ref, kc_ref, vc_ref, bias_ref, o_ref, *, rb, band, rows, hp, bounded):
    i = pl.program_id(2)
    ub = jnp.clip(i * rb - WIN_R // 2, 0, rows - band)
    off = pl.multiple_of(ub * GRID_W, GRID_W)
    heads = lambda t: jnp.stack([t[:, h * DH_C:(h + 1) * DH_C] for h in range(hp)])
    q = heads(q_ref[0])
    kb = heads(k_ref[0, pl.ds(off, band * GRID_W), :])
    vb = heads(v_ref[0, pl.ds(off, band * GRID_W), :])
    kc = heads(kc_ref[0])
    vc = heads(vc_ref[0])
    bqk = lambda a, b_: jnp.einsum('hqd,hkd->hqk', a, b_, preferred_element_type=f32)
    bpv = lambda a, b_: jnp.einsum('hqk,hkd->hqd', a, b_, preferred_element_type=f32)
    s_lat = bqk(q, kb) + bias_ref[0]
    s_ctx = bqk(q, kc)
    if bounded:
        def with_ones(t):
            ones_col = jnp.where(lax.broadcasted_iota(jnp.int32, t.shape, 2) == 0, 1.0, 0.0).astype(bf16)
            return jnp.concatenate([t, ones_col], axis=2)

        bound = sc_ref[0]
        acc = (bpv(jnp.exp2(s_lat - bound).astype(bf16), with_ones(vb))
               + bpv(jnp.exp2(s_ctx - bound).astype(bf16), with_ones(vc)))
        o = acc[:, :, :DH_C] * (1.0 / jnp.sum(acc[:, :, DH_C:], axis=-1, keepdims=True))
    else:
        m = jnp.maximum(jnp.max(s_lat, axis=-1, keepdims=True), jnp.max(s_ctx, axis=-1, keepdims=True))
        p_lat = jnp.exp2(s_lat - m)
        p_ctx = jnp.exp2(s_ctx - m)
        den = jnp.sum(p_lat, axis=-1, keepdims=True) + jnp.sum(p_ctx, axis=-1, keepdims=True)
        o = (bpv(p_lat.astype(bf16), vb) + bpv(p_ctx.astype(bf16), vc)) * (1.0 / den)
    o_ref[0] = jnp.concatenate([o[h] for h in range(hp)], axis=-1).astype(o_ref.dtype)


def _na_bias_index(rows, rb, band):
    wr = min(WIN_R, rows)
    nblk = rows // rb
    cols = np.arange(GRID_W)
    c_start = np.clip(cols - WIN_C // 2, 0, GRID_W - WIN_C)
    col_ok = (cols[None, :] >= c_start[:, None]) & (cols[None, :] < c_start[:, None] + WIN_C)
    dc_idx = np.clip(cols[None, :] - cols[:, None] + WIN_C - 1, 0, 2 * WIN_C - 2)

    def geometry(i):
        ub = int(np.clip(i * rb - WIN_R // 2, 0, rows - band))
        qr = i * rb + np.arange(rb)
        kr = ub + np.arange(band)
        r_start = np.clip(qr - wr // 2, 0, rows - wr)
        row_ok = (kr[None, :] >= r_start[:, None]) & (kr[None, :] < r_start[:, None] + wr)
        dr = np.clip(kr[None, :] - qr[:, None] + WIN_R - 1, 0, 2 * WIN_R - 2)
        ok = row_ok[:, None, :, None] & col_ok[None, :, None, :]
        return ok.reshape(rb * GRID_W, band * GRID_W), dr

    reps = [0, min(1, nblk - 1), nblk - 1]
    geo = [geometry(i) for i in reps]
    for i in range(1, nblk - 1):
        g = geometry(i)
        assert all(np.array_equal(a, b_) for a, b_ in zip(g, geo[1]))
    ok = np.stack([g[0] for g in geo])
    dr = np.stack([g[1] for g in geo])
    dr_onehot = (dr[..., None] == np.arange(2 * WIN_R - 1)).astype(np.float32)
    dc_onehot = (dc_idx[None] == np.arange(2 * WIN_C - 1)[:, None, None]).astype(np.float32)
    return ok, dr_onehot, dc_onehot


def _na_bias_table(rpb, rows, rb, band):
    ok, dr_onehot, dc_onehot = _na_bias_index(rows, rb, band)
    hi = lax.Precision.HIGHEST
    by_col = jnp.einsum('hrc,cqk->hrqk', rpb.astype(f32), dc_onehot, precision=hi)
    slabs = jnp.einsum('vabr,hrqk->vhaqbk', dr_onehot, by_col, precision=hi)
    h = rpb.shape[0]
    return jnp.where(ok[:, None], slabs.reshape(3, h, rb * GRID_W, band * GRID_W) * math.log2(math.e), NEG)


def neighbourhood_attention(q, k, v, k_ctx, v_ctx, rpb, bound, bounded, rb=4):
    b, l, _ = q.shape
    lc = k_ctx.shape[1]
    rows = l // GRID_W
    band = rb + WIN_R - 1
    assert rows % rb == 0 and rows >= band and WIN_R <= rows
    nblk = rows // rb
    bias = _na_bias_table(rpb, rows, rb, band)
    tq = rb * GRID_W
    variant = lambda i: jnp.where(i == 0, 0, jnp.where(i == nblk - 1, 2, 1))
    hp = 4
    assert H_C % hp == 0
    hw = hp * DH_C
    head_blk = lambda n: pl.BlockSpec((1, n, hw), lambda bi, h, i: (bi, 0, h))
    return pl.pallas_call(
        functools.partial(_na_kernel, rb=rb, band=band, rows=rows, hp=hp, bounded=bounded),
        out_shape=jax.ShapeDtypeStruct((b, l, H_C * DH_C), bf16),
        grid=(b, H_C // hp, nblk),
        in_specs=[
            pl.BlockSpec(memory_space=pltpu.SMEM),
            pl.BlockSpec((1, tq, hw), lambda bi, h, i: (bi, i, h)),
            head_blk(l), head_blk(l), head_blk(lc), head_blk(lc),
            pl.BlockSpec((1, hp, tq, band * GRID_W), lambda bi, h, i: (variant(i), h, 0, 0)),
        ],
        out_specs=pl.BlockSpec((1, tq, hw), lambda bi, h, i: (bi, i, h)),
        compiler_params=_cparams(("parallel", "parallel", "arbitrary")),
        name="na_attn" if bounded else "na_attn_online",
    )(bound, q, k, v, k_ctx, v_ctx, bias)


def _outproj_kernel(*refs, has_gdn):
    if has_gdn:
        x_ref, g_ref, a_ref, wa_ref, of_ref, ob_ref, gate_ref, gain_ref, wb_ref, o_ref = refs
    else:
        x_ref, g_ref, a_ref, wa_ref, o_ref = refs
    acc = _dot(a_ref[0], wa_ref[...])
    if has_gdn:
        o = of_ref[0].astype(f32) + ob_ref[0].astype(f32)
        gate = gate_ref[0].astype(f32)
        parts = []
        for h in range(H_B):
            sl = slice(h * DH_B, (h + 1) * DH_B)
            t = o[:, sl]
            y = t * lax.rsqrt(jnp.mean(t * t, axis=-1, keepdims=True) + EPS) * gain_ref[...]
            parts.append((y * _silu(gate[:, sl])).astype(bf16))
        acc = acc + _dot(jnp.concatenate(parts, axis=-1), wb_ref[...])
    o_ref[0] = x_ref[0] + g_ref[0] * acc


def out_project(x, g, a, wa, gdn=None, tm=512, name="outproj"):
    b, l, d = x.shape
    tm = min(tm, l)
    assert l % tm == 0
    bm = g.shape[0]
    mod_map = (lambda bi, i: (bi, 0, 0)) if bm == b else (lambda bi, i: (0, 0, 0))
    row = lambda n: pl.BlockSpec((1, tm, n), lambda bi, i: (bi, i, 0))
    full = lambda arr: pl.BlockSpec(arr.shape, lambda bi, i: (0, 0))
    in_specs = [row(d), pl.BlockSpec((1, 1, d), mod_map), row(a.shape[2]), full(wa)]
    args = [x, g, a, wa]
    if gdn is not None:
        o_f, o_b, gate, gain, wb = gdn
        in_specs += [row(B_W), row(B_W), row(B_W), full(gain), full(wb)]
        args += [o_f, o_b, gate, gain, wb]
    return pl.pallas_call(
        functools.partial(_outproj_kernel, has_gdn=gdn is not None),
        out_shape=jax.ShapeDtypeStruct((b, l, d), f32),
        grid=(b, l // tm),
        in_specs=in_specs,
        out_specs=row(d),
        compiler_params=_cparams(("parallel", "parallel")),
        name=name,
    )(*args)


def _ffn_kernel(x_ref, gain_ref, sc_ref, sh_ref, g_ref, win_ref, wo_ref, o_ref, h_ref, acc_ref, *, tf):
    h_ref[...] = _norm_mod(x_ref[0], gain_ref[...], sc_ref[0], sh_ref[0]).astype(bf16)
    acc_ref[...] = jnp.zeros_like(acc_ref)
    ff = wo_ref.shape[0]
    tm = h_ref.shape[0]
    halves = [slice(r * (tm // 2), (r + 1) * (tm // 2)) for r in range(2)] if tm >= 512 else [slice(0, tm)]

    def body(j, carry):
        c0 = pl.multiple_of(j * tf, tf)
        wg = win_ref[:, pl.ds(c0, tf)]
        wu = win_ref[:, pl.ds(ff + c0, tf)]
        wo = wo_ref[pl.ds(c0, tf), :]
        gate_up = []
        for rows in halves:
            h = h_ref[rows, :]
            gate_up.append((_dot(h, wg), _dot(h, wu)))
        for rows, (gt, up) in zip(halves, gate_up):
            acc_ref[rows, :] += _dot((_silu(gt) * up).astype(bf16), wo)
        return carry

    lax.fori_loop(0, ff // tf, body, 0)
    o_ref[0] = x_ref[0] + g_ref[0] * acc_ref[...]


def ffn(x, gain, sc, sh, g, w_in, w_out, tm=1024, tf=256, name="ffn"):
    b, l, d = x.shape
    ff = w_out.shape[0]
    tm = min(tm, l)
    assert l % tm == 0 and ff % tf == 0
    bm = sc.shape[0]
    mod_map = (lambda bi, i: (bi, 0, 0)) if bm == b else (lambda bi, i: (0, 0, 0))
    mod = pl.BlockSpec((1, 1, d), mod_map)
    resident = lambda arr: pl.BlockSpec(arr.shape, lambda bi, i: (0, 0), pipeline_mode=pl.Buffered(1))
    return pl.pallas_call(
        functools.partial(_ffn_kernel, tf=tf),
        out_shape=jax.ShapeDtypeStruct((b, l, d), f32),
        grid=(b, l // tm),
        in_specs=[
            pl.BlockSpec((1, tm, d), lambda bi, i: (bi, i, 0)),
            pl.BlockSpec((1, d), lambda bi, i: (0, 0)),
            mod, mod, mod,
            resident(w_in), resident(w_out),
        ],
        out_specs=pl.BlockSpec((1, tm, d), lambda bi, i: (bi, i, 0)),
        scratch_shapes=[pltpu.VMEM((tm, d), bf16), pltpu.VMEM((tm, d), f32)],
        compiler_params=_cparams(("parallel", "parallel")),
        name=name,
    )(x, gain, sc, sh, g, w_in, w_out)


def _rope_tables(l):
    n_freq = DH_A // 4
    inv = ROPE_BASE ** (-jnp.arange(n_freq, dtype=f32) / n_freq)
    t = jnp.arange(l)
    row = (t // GRID_W).astype(f32)
    col = (t % GRID_W).astype(f32)
    ang = jnp.concatenate([row[:, None] * inv, col[:, None] * inv], axis=-1)
    cos = jnp.repeat(jnp.cos(ang), 2, axis=-1)
    sin = jnp.repeat(jnp.sin(ang), 2, axis=-1)
    sign = jnp.tile(jnp.array([-1.0, 1.0], f32), DH_A // 2)
    return jnp.tile(cos, (1, 2)), jnp.tile(sin * sign, (1, 2))


def _pad_rows(a, rows):
    return jnp.zeros((rows,) + a.shape[1:], a.dtype).at[:a.shape[0]].set(a)


def kernel(x, c, ctx, c_ctx, ada_w, ada_b, norm_mix, norm_ffn, ffn_w_in, ffn_w_out, even_w_in, even_w_out,
           diff_qk_gain, diff_lambda, diff_subln, gdn_conv, gdn_a_log, gdn_dt_bias, gdn_norm, odd_w_in,
           odd_w_out, na_qk_gain, na_rpb):
    b, l, d = x.shape
    lc = ctx.shape[1]
    depth = ada_w.shape[0]
    c_all = _pad_rows(jnp.concatenate([c, c_ctx[None]], axis=0), 16)
    mods = ada_modulation(c_all, ada_w, ada_b)
    cos_t, sin_t = _rope_tables(l)
    ctx_flat = ctx.reshape(1, b * lc, d)

    for layer in range(depth):
        ctx_out = layer < depth - 1
        m_lat = [t[:, None, :] for t in jnp.split(mods[layer, :b], 6, axis=-1)]
        m_ctx = [t[:, None, :] for t in jnp.split(mods[layer, b:b + 1], 6, axis=-1)]
        sh_m, sc_m, g_m, sh_f, sc_f, g_f = m_lat
        csh_m, csc_m, cg_m, csh_f, csc_f, cg_f = m_ctx
        gain_m = norm_mix[layer][None]
        gain_f = norm_ffn[layer][None]
        if layer % 2 == 0:
            e = layer // 2
            lam_init = 0.8 - 0.6 * math.exp(-0.3 * layer)
            w_in = even_w_in[e].astype(bf16)
            cuts = [0, A_W, 2 * A_W, 3 * A_W, 3 * A_W + 3 * B_W, 3 * A_W + 4 * B_W]
            ws = [w_in[:, cuts[i]:cuts[i + 1]] for i in range(5)]
            w_gates = jnp.zeros((d, LANES), bf16).at[:, :4 * H_B].set(w_in[:, cuts[5]:])
            ws.append(w_gates)
            qk_gain = _pad_rows(jnp.tile(diff_qk_gain[e], (1, 2)), 8)
            dts = [bf16, bf16, bf16, f32, bf16, f32]
            q_scale = DH_A ** -0.5 * math.log2(math.e)
            epi = lambda rope: [("submap", 0, rope, q_scale), ("submap", 1, rope, 1.0), ("plain",),
                                ("plain",), ("plain",), ("plain",)]
            qa, ka, va, qkv_b, g_b, gates = norm_mod_project(
                x, gain_m, sc_m, sh_m, ws, dts, epi(True), qk_gain, rope_tabs=(cos_t, sin_t), name="even_proj")
            qac, kac, vac, qkv_bc, g_bc, gates_c = norm_mod_project(
                ctx_flat, gain_m, csc_m, csh_m, ws, dts, epi(False), qk_gain, name="even_proj_ctx")
            unflat = lambda t: t.reshape(b, lc, t.shape[-1])
            qac, kac, vac, qkv_bc, g_bc, gates_c = map(unflat, (qac, kac, vac, qkv_bc, g_bc, gates_c))

            lv = diff_lambda[e]
            lam = jnp.exp(jnp.sum(lv[0] * lv[1])) - jnp.exp(jnp.sum(lv[2] * lv[3])) + lam_init
            gmax = jnp.max(jnp.abs(diff_qk_gain[e]), axis=-1)
            bound = 1.02 * DH_A * q_scale * gmax[0] * gmax[1]
            scal = jnp.stack([lam, bound]).astype(f32)
            subln = diff_subln[e][None]
            attn_args = (scal, qa, kac, vac, ka, va, subln, lam_init)
            a_lat = lax.cond(2.0 * bound < 120.0,
                             lambda: diff_attention_bounded(*attn_args),
                             lambda: diff_attention(*attn_args, name="diff_attn_online"))
            a_ctx = diff_attention(scal, qac, kac, vac, None, None, subln, lam_init, name="diff_attn_ctx")

            y_lat = gdn_short_conv(qkv_b, gdn_conv[e])
            y_ctx = gdn_short_conv(qkv_bc, gdn_conv[e])
            per_pos = lambda p: jnp.repeat(p, GDN_CHUNK)
            prm = jnp.stack([per_pos(gdn_a_log[e][0]), per_pos(gdn_dt_bias[e][0]),
                             per_pos(gdn_a_log[e][1]), per_pos(gdn_dt_bias[e][1])])

            def to_rows(g):
                t = g[:, :, :4 * H_B].reshape(g.shape[0], -1, GDN_CHUNK, 4, H_B).transpose(0, 1, 3, 4, 2)
                t = t.reshape(g.shape[0], -1, 4, H_B * GDN_CHUNK)
                return t[:, :, jnp.array([0, 2, 1, 3])]

            s0 = jnp.zeros((b, 2, H_B * DH_B, DH_B), f32)
            ocf, ocb, s_mid = gdn_scan(y_ctx, to_rows(gates_c), prm, s0)
            olf, olb, _ = gdn_scan(y_lat, to_rows(gates), prm, s_mid)

            w_out = even_w_out[e].astype(bf16)
            gdn_gain = gdn_norm[e][None]
            x = out_project(x, g_m, a_lat, w_out[:A_W], gdn=(olf, olb, g_b, gdn_gain, w_out[A_W:]),
                            name="even_out")
            ctx_flat = out_project(
                ctx_flat, cg_m, a_ctx.reshape(1, b * lc, A_W), w_out[:A_W],
                gdn=(ocf.reshape(1, b * lc, B_W), ocb.reshape(1, b * lc, B_W), g_bc.reshape(1, b * lc, B_W),
                     gdn_gain, w_out[A_W:]), name="even_out_ctx")
        else:
            od = layer // 2
            w_in = odd_w_in[od].astype(bf16)
            mix = H_C * DH_C
            ws = [w_in[:, :mix], w_in[:, mix:2 * mix], w_in[:, 2 * mix:]]
            qk_gain = _pad_rows(na_qk_gain[od], 8)
            q_scale = DH_C ** -0.5 * math.log2(math.e)
            q, k, v = norm_mod_project(
                x, gain_m, sc_m, sh_m, ws, [bf16] * 3,
                [("headnorm", 0, q_scale), ("headnorm", 1, 1.0), ("plain",)], qk_gain, name="odd_proj")
            kc, vc = norm_mod_project(
                ctx_flat, gain_m, csc_m, csh_m, ws[1:], [bf16] * 2,
                [("headnorm", 1, 1.0), ("plain",)], qk_gain, name="odd_proj_ctx")
            kc, vc = kc.reshape(b, lc, mix), vc.reshape(b, lc, mix)
            gmax = jnp.max(jnp.abs(na_qk_gain[od]), axis=-1)
            bound = (1.02 * DH_C * q_scale * gmax[0] * gmax[1]
                     + jnp.max(jnp.abs(na_rpb[od])) * math.log2(math.e)).astype(f32).reshape(1)
            na = functools.partial(neighbourhood_attention, q, k, v, kc, vc, na_rpb[od], bound)
            o = lax.cond(2.0 * bound[0] < 120.0, lambda: na(True), lambda: na(False))
            x = out_project(x, g_m, o, odd_w_out[od].astype(bf16), name="odd_out")
            if ctx_out:
                raise NotImplementedError("context output of a neighbourhood layer is not needed at depth 2")

        wf_in = ffn_w_in[layer].astype(bf16)
        wf_out = ffn_w_out[layer].astype(bf16)
        x = ffn(x, gain_f, sc_f, sh_f, g_f, wf_in, wf_out, name="ffn")
        if ctx_out:
            ctx_flat = ffn(ctx_flat, gain_f, csc_f, csh_f, cg_f, wf_in, wf_out, name="ffn_ctx")
    return x
```
